```python
import jax, jax.numpy as jnp
from jax import lax
import numpy as np

D_MODEL = 1024
BATCH = 16
SEQ = 4096
DEPTH = 1
DEC_BATCH = 8
DEC_SEQ = 16
PAST_LEN = 2048

CHUNK = 64
GMLP_CHUNK = 128
GMLP_WIDTH = 1024
GMLP_GROUPS = 4
GMLP_GROUP_DIM = GMLP_WIDTH // GMLP_GROUPS
CONV_WIDTH = 1024
CONV_K = 31
MEM_LEN = 256
MEM_HEADS = 4
MEM_HEAD_DIM = 256
MEM_WIDTH = MEM_HEADS * MEM_HEAD_DIM
N_BRANCH = 3
IN_COLS = 2 * GMLP_WIDTH + 2 * CONV_WIDTH + MEM_WIDTH + N_BRANCH * D_MODEL
PEER_HEADS = 8
PEER_NKEYS = 128
PEER_EXPERTS = PEER_NKEYS * PEER_NKEYS
PEER_DKEY = 256
PEER_TOPK = 16
PEER_BLOCK = 256
EPS = 1e-6

kernel_name = 'gated_gmlp_conformer_memxattn_peer_stream'


def rmsnorm(x, g):
    xf = x.astype(jnp.float32)
    r = lax.rsqrt(jnp.mean(xf * xf, axis=-1, keepdims=True) + EPS)
    return (xf * r).astype(x.dtype) * g


def layernorm(x, g, b):
    xf = x.astype(jnp.float32)
    mu = jnp.mean(xf, axis=-1, keepdims=True)
    var = jnp.mean(jnp.square(xf - mu), axis=-1, keepdims=True)
    return ((xf - mu) * lax.rsqrt(var + EPS)).astype(x.dtype) * g + b


def gmlp_spatial(vn, ws, bs):
    B, T, _ = vn.shape
    n = min(T, GMLP_CHUNK)
    pos = jnp.arange(n)
    mask = (pos[None, :] // CHUNK <= pos[:, None] // CHUNK).astype(vn.dtype)
    w = ws[:, :n, :n] * mask[None]
    vg = vn.reshape(B, T // n, n, GMLP_GROUPS, GMLP_GROUP_DIM)
    out = jnp.einsum('gij,bcjgd->bcigd', w, vg) + jnp.transpose(bs[:, :n])[None, None, :, :, None]
    return out.reshape(B, T, GMLP_WIDTH)


def depthwise_causal_conv(xpad, w, b):
    C = xpad.shape[-1]
    y = lax.conv_general_dilated(xpad, w[:, None, :], window_strides=(1,), padding='VALID',
                                 dimension_numbers=('NWC', 'WIO', 'NWC'), feature_group_count=C)
    return y + b


def memory_kv(mem, norm_mem_g, w_mem_kv):
    B = mem.shape[0]
    kv = (rmsnorm(mem, norm_mem_g) @ w_mem_kv).reshape(B, MEM_LEN, 2, MEM_HEADS, MEM_HEAD_DIM)
    return kv[:, :, 0], kv[:, :, 1]


def peer(h, wq, k1, k2, U, V):
    B, T, D = h.shape
    flat = h.reshape(-1, D)
    n = flat.shape[0]
    nblk = -(-n // PEER_BLOCK)
    flat = jnp.pad(flat, ((0, nblk * PEER_BLOCK - n), (0, 0))).reshape(nblk, PEER_BLOCK, D)
    half = PEER_DKEY // 2

    def block(xb):
        q = (xb @ wq).reshape(PEER_BLOCK, PEER_HEADS, PEER_DKEY)
        s1 = jnp.einsum('thd,nd->thn', q[..., :half], k1).astype(jnp.float32)
        s2 = jnp.einsum('thd,nd->thn', q[..., half:], k2).astype(jnp.float32)
        v1, i1 = lax.top_k(s1, PEER_TOPK)
        v2, i2 = lax.top_k(s2, PEER_TOPK)
        cand = (v1[..., :, None] + v2[..., None, :]).reshape(PEER_BLOCK, PEER_HEADS, PEER_TOPK * PEER_TOPK)
        cidx = (i1[..., :, None] * PEER_NKEYS + i2[..., None, :]).reshape(PEER_BLOCK, PEER_HEADS, PEER_TOPK * PEER_TOPK)
        sv, sel = lax.top_k(cand, PEER_TOPK)
        eidx = jnp.take_along_axis(cidx, sel, axis=-1)
        g = jax.nn.softmax(sv, axis=-1).astype(xb.dtype)
        a = jax.nn.gelu(jnp.einsum('td,thkd->thk', xb, U[eidx]))
        return jnp.einsum('thk,thkd->td', g * a, V[eidx])

    out = lax.map(block, flat).reshape(-1, D)[:n]
    return out.reshape(B, T, D)


def layer(x, conv_hist, mem_k, mem_v, norm_mix_g, w_in, b_gate, gmlp_ln_g, gmlp_ln_b, gmlp_ws, gmlp_bs,
          w_a_out, conv_w, conv_b, conv_ln_g, conv_ln_b, w_b_out, w_c_out, w_o, norm_ffn_g,
          peer_wq, peer_k1, peer_k2, peer_u, peer_v):
    B, T, D = x.shape
    h = rmsnorm(x, norm_mix_g)
    z = h @ w_in
    c1 = 2 * GMLP_WIDTH
    c2 = c1 + 2 * CONV_WIDTH
    c3 = c2 + MEM_WIDTH
    zA, zB, zC, zG = z[..., :c1], z[..., c1:c2], z[..., c2:c3], z[..., c3:]
    zA = jax.nn.gelu(zA)
    u, v = zA[..., :GMLP_WIDTH], zA[..., GMLP_WIDTH:]
    vn = layernorm(v, gmlp_ln_g, gmlp_ln_b)
    yA = (u * gmlp_spatial(vn, gmlp_ws, gmlp_bs)) @ w_a_out
    glu = zB[..., :CONV_WIDTH] * jax.nn.sigmoid(zB[..., CONV_WIDTH:])
    full = jnp.concatenate([conv_hist, glu], axis=1)
    c = depthwise_causal_conv(full, conv_w, conv_b)
    yB = jax.nn.silu(layernorm(c, conv_ln_g, conv_ln_b)) @ w_b_out
    new_conv = full[:, -(CONV_K - 1):]
    q = zC.reshape(B, T, MEM_HEADS, MEM_HEAD_DIM)
    s = jnp.einsum('bthd,bmhd->bhtm', q, mem_k).astype(jnp.float32) * (MEM_HEAD_DIM ** -0.5)
    pr = jax.nn.softmax(s, axis=-1).astype(x.dtype)
    yC = jnp.einsum('bhtm,bmhd->bthd', pr, mem_v).reshape(B, T, MEM_WIDTH) @ w_c_out
    gates = jax.nn.sigmoid(zG + b_gate).reshape(B, T, N_BRANCH, D)
    merged = gates[:, :, 0] * yA + gates[:, :, 1] * yB + gates[:, :, 2] * yC
    x = x + merged @ w_o
    x = x + peer(rmsnorm(x, norm_ffn_g), peer_wq, peer_k1, peer_k2, peer_u, peer_v)
    return x, new_conv, vn


def setup_inputs(seed: int = 0) -> dict:
    key = jax.random.key(seed)
    ks = iter(jax.random.split(key, 40))

    def nrm(shape, scale):
        return jax.random.normal(next(ks), shape, jnp.float32) * scale

    L = DEPTH
    return {
        'x_prompt': nrm((BATCH, SEQ, D_MODEL), 1.0),
        'mem_prompt': nrm((BATCH, MEM_LEN, D_MODEL), 1.0),
        'x_sample': nrm((DEC_BATCH, DEC_SEQ, D_MODEL), 1.0),
        'cache_mem_k': nrm((L, DEC_BATCH, MEM_LEN, MEM_HEADS, MEM_HEAD_DIM), 1.0),
        'cache_mem_v': nrm((L, DEC_BATCH, MEM_LEN, MEM_HEADS, MEM_HEAD_DIM), 1.0),
        'state_conv': nrm((L, DEC_BATCH, CONV_K - 1, CONV_WIDTH), 0.5),
        'norm_mix_g': 1.0 + nrm((L, D_MODEL), 0.01),
        'w_in': nrm((L, D_MODEL, IN_COLS), D_MODEL ** -0.5),
        'b_gate': nrm((L, N_BRANCH * D_MODEL), 0.01),
        'gmlp_ln_g': 1.0 + nrm((L, GMLP_WIDTH), 0.01),
        'gmlp_ln_b': nrm((L, GMLP_WIDTH), 0.01),
        'gmlp_ws': nrm((L, GMLP_GROUPS, GMLP_CHUNK, GMLP_CHUNK), GMLP_CHUNK ** -0.5),
        'gmlp_bs': 1.0 + nrm((L, GMLP_GROUPS, GMLP_CHUNK), 0.01),
        'w_a_out': nrm((L, GMLP_WIDTH, D_MODEL), GMLP_WIDTH ** -0.5),
        'conv_w': nrm((L, CONV_K, CONV_WIDTH), CONV_K ** -0.5),
        'conv_b': nrm((L, CONV_WIDTH), 0.01),
        'conv_ln_g': 1.0 + nrm((L, CONV_WIDTH), 0.01),
        'conv_ln_b': nrm((L, CONV_WIDTH), 0.01),
        'w_b_out': nrm((L, CONV_WIDTH, D_MODEL), CONV_WIDTH ** -0.5),
        'norm_mem_g': 1.0 + nrm((L, D_MODEL), 0.01),
        'w_mem_kv': nrm((L, D_MODEL, 2 * MEM_WIDTH), D_MODEL ** -0.5),
        'w_c_out': nrm((L, MEM_WIDTH, D_MODEL), MEM_WIDTH ** -0.5),
        'w_o': nrm((L, D_MODEL, D_MODEL), D_MODEL ** -0.5),
        'norm_ffn_g': 1.0 + nrm((L, D_MODEL), 0.01),
        'peer_wq': nrm((L, D_MODEL, PEER_HEADS * PEER_DKEY), D_MODEL ** -0.5),
        'peer_k1': nrm((L, PEER_NKEYS, PEER_DKEY // 2), (PEER_DKEY // 2) ** -0.5),
        'peer_k2': nrm((L, PEER_NKEYS, PEER_DKEY // 2), (PEER_DKEY // 2) ** -0.5),
        'peer_u': nrm((L, PEER_EXPERTS, D_MODEL), D_MODEL ** -0.5),
        'peer_v': nrm((L, PEER_EXPERTS, D_MODEL), PEER_HEADS ** -0.5),
        'norm_out_g': 1.0 + nrm((D_MODEL,), 0.01),
    }


def reference(x_prompt, mem_prompt, x_sample, cache_mem_k, cache_mem_v, state_conv,
              norm_mix_g, w_in, b_gate, gmlp_ln_g, gmlp_ln_b, gmlp_ws, gmlp_bs, w_a_out,
              conv_w, conv_b, conv_ln_g, conv_ln_b, w_b_out, norm_mem_g, w_mem_kv, w_c_out, w_o,
              norm_ffn_g, peer_wq, peer_k1, peer_k2, peer_u, peer_v, norm_out_g):
    xp, xs = x_prompt, x_sample
    conv_p_l, mk_l, mv_l, conv_s_l, vs_l = [], [], [], [], []
    for l in range(DEPTH):
        lw = (norm_mix_g[l], w_in[l], b_gate[l], gmlp_ln_g[l], gmlp_ln_b[l], gmlp_ws[l], gmlp_bs[l],
              w_a_out[l], conv_w[l], conv_b[l], conv_ln_g[l], conv_ln_b[l], w_b_out[l], w_c_out[l], w_o[l],
              norm_ffn_g[l], peer_wq[l], peer_k1[l], peer_k2[l], peer_u[l], peer_v[l])
        mk_p, mv_p = memory_kv(mem_prompt, norm_mem_g[l], w_mem_kv[l])
        hist0 = jnp.zeros((xp.shape[0], CONV_K - 1, CONV_WIDTH), xp.dtype)
        xp, conv_p, _ = layer(xp, hist0, mk_p, mv_p, *lw)
        xs, conv_s, v_s = layer(xs, state_conv[l], cache_mem_k[l], cache_mem_v[l], *lw)
        conv_p_l.append(conv_p)
        mk_l.append(mk_p)
        mv_l.append(mv_p)
        conv_s_l.append(conv_s)
        vs_l.append(v_s)
    y_prompt = rmsnorm(xp, norm_out_g)
    y_sample = rmsnorm(xs, norm_out_g)
    return (y_prompt, y_sample, jnp.stack(conv_p_l), jnp.stack(mk_l), jnp.stack(mv_l),
            jnp.stack(conv_s_l), jnp.stack(vs_l))
```

```python
import functools

import jax
import jax.numpy as jnp
from jax import lax
from jax.experimental import pallas as pl
from jax.experimental.pallas import tpu as pltpu

EPS = 1e-6
CHUNK = 64
GMLP_CHUNK = 128
GMLP_GROUPS = 4
CONV_K = 31
MEM_HEADS = 4
PEER_HEADS = 8
PEER_NKEYS = 128
PEER_TOPK = 16
HALO = 32

F32 = jnp.float32
BF16 = jnp.bfloat16
VMEM_LIMIT = 48 * 1024 * 1024


def _cparams(sem):
    return pltpu.CompilerParams(dimension_semantics=sem, vmem_limit_bytes=VMEM_LIMIT)


def _rms(x, g):
    r = lax.rsqrt(jnp.mean(x * x, axis=-1, keepdims=True) + EPS)
    return (x * r) * g


def _layernorm(x, g, b):
    mu = jnp.mean(x, axis=-1, keepdims=True)
    d = x - mu
    var = jnp.mean(d * d, axis=-1, keepdims=True)
    return (d * lax.rsqrt(var + EPS)) * g + b


def _dot(a, b):
    return jnp.dot(a, b, preferred_element_type=F32)


def _full(shape):
    n = len(shape)
    return pl.BlockSpec(shape, lambda *_: (0,) * n)


def _norm_matmul_kernel(x_ref, g_ref, w_ref, o_ref):
    h = _rms(x_ref[...], g_ref[...]).astype(BF16)
    o_ref[...] = _dot(h, w_ref[...])


def _norm_matmul(x, g, w, tm):
    n, d = x.shape
    m = w.shape[1]
    return pl.pallas_call(
        _norm_matmul_kernel,
        grid=(n // tm,),
        in_specs=[pl.BlockSpec((tm, d), lambda i: (i, 0)), _full((1, d)), _full((d, m))],
        out_specs=pl.BlockSpec((tm, m), lambda i: (i, 0)),
        out_shape=jax.ShapeDtypeStruct((n, m), F32),
        compiler_params=_cparams(("parallel",)),
        name="mem_kv",
    )(x, g, w)


def _proj_a_kernel(x_ref, g_ref, w_ref, lg_ref, lb_ref, u_ref, vn_ref):
    h = _rms(x_ref[...], g_ref[...]).astype(BF16)
    d = u_ref.shape[1]
    u_ref[...] = jax.nn.gelu(_dot(h, w_ref[:, :d])).astype(u_ref.dtype)
    v = jax.nn.gelu(_dot(h, w_ref[:, d:]))
    vn_ref[...] = _layernorm(v, lg_ref[...], lb_ref[...])


def _proj_b_kernel(x_ref, g_ref, w_ref, glu_ref):
    h = _rms(x_ref[...], g_ref[...]).astype(BF16)
    d = glu_ref.shape[1]
    glu_ref[...] = _dot(h, w_ref[:, :d]) * jax.nn.sigmoid(_dot(h, w_ref[:, d:]))


def _proj_cg_kernel(x_ref, g_ref, w_ref, bg_ref, q_ref, gates_ref):
    h = _rms(x_ref[...], g_ref[...]).astype(BF16)
    d = q_ref.shape[1]
    q_ref[...] = _dot(h, w_ref[:, :d]).astype(q_ref.dtype)
    for k in range(gates_ref.shape[1] // d):
        z = _dot(h, w_ref[:, (k + 1) * d:(k + 2) * d]) + bg_ref[:, k * d:(k + 1) * d]
        gates_ref[:, k * d:(k + 1) * d] = jax.nn.sigmoid(z).astype(gates_ref.dtype)


def _in_proj(x, g_mix, w_a, w_b, w_cg, b_gate, ln_g, ln_b, tm):
    n, d = x.shape
    row = lambda w: pl.BlockSpec((tm, w), lambda i: (i, 0))
    grid = (n // tm,)
    u, vn = pl.pallas_call(
        _proj_a_kernel, grid=grid,
        in_specs=[row(d), _full((1, d)), _full(w_a.shape), _full((1, d)), _full((1, d))],
        out_specs=[row(d), row(d)],
        out_shape=[jax.ShapeDtypeStruct((n, d), BF16), jax.ShapeDtypeStruct((n, d), F32)],
        compiler_params=_cparams(("parallel",)), name="in_proj_gmlp",
    )(x, g_mix, w_a, ln_g, ln_b)
    glu = pl.pallas_call(
        _proj_b_kernel, grid=grid,
        in_specs=[row(d), _full((1, d)), _full(w_b.shape)],
        out_specs=row(d),
        out_shape=jax.ShapeDtypeStruct((n, d), F32),
        compiler_params=_cparams(("parallel",)), name="in_proj_conv",
    )(x, g_mix, w_b)
    ng = b_gate.shape[1]
    q, gates = pl.pallas_call(
        _proj_cg_kernel, grid=grid,
        in_specs=[row(d), _full((1, d)), _full(w_cg.shape), _full((1, ng))],
        out_specs=[row(d), row(ng)],
        out_shape=[jax.ShapeDtypeStruct((n, d), BF16), jax.ShapeDtypeStruct((n, ng), BF16)],
        compiler_params=_cparams(("parallel",)), name="in_proj_attn_gates",
    )(x, g_mix, w_cg, b_gate)
    return u, vn, glu, q, gates


def _mixer_kernel(x_ref, u_ref, vn_ref, glu_ref, halo_ref, hist_ref, q_ref, gates_ref, mk_ref, mv_ref,
                  ws_ref, bs_ref, wa_ref, cw_ref, cb_ref, clg_ref, clb_ref, wb_ref, wc_ref, wo_ref,
                  o_ref, ext_ref, mix_ref, *, n_valid):
    tq, d = x_ref.shape
    i = pl.program_id(1)

    @pl.when(i == 0)
    def _():
        ext_ref[0:HALO, :] = hist_ref[...]

    @pl.when(i > 0)
    def _():
        ext_ref[0:HALO, :] = halo_ref[...]

    ext_ref[HALO:HALO + tq, :] = glu_ref[...]
    off = HALO - (CONV_K - 1)
    c = cb_ref[...] + cw_ref[0:1, :] * ext_ref[off:off + tq, :]
    for k in range(1, CONV_K):
        c = c + cw_ref[k:k + 1, :] * ext_ref[off + k:off + k + tq, :]
    yb_in = jax.nn.silu(_layernorm(c, clg_ref[...], clb_ref[...])).astype(BF16)
    y_b = _dot(yb_in, wb_ref[...])

    ii = lax.broadcasted_iota(jnp.int32, (GMLP_CHUNK, GMLP_CHUNK), 0)
    jj = lax.broadcasted_iota(jnp.int32, (GMLP_CHUNK, GMLP_CHUNK), 1)
    shift = CHUNK.bit_length() - 1
    mask = ((jj >> shift) <= (ii >> shift)) & (ii < n_valid) & (jj < n_valid)
    gd = d // GMLP_GROUPS
    for g in range(GMLP_GROUPS):
        wg = jnp.where(mask, ws_ref[g], 0.0).astype(BF16)
        for cgrp in range(tq // GMLP_CHUNK):
            rs = slice(cgrp * GMLP_CHUNK, (cgrp + 1) * GMLP_CHUNK)
            cs = slice(g * gd, (g + 1) * gd)
            sp = _dot(wg, vn_ref[rs, cs].astype(BF16)) + bs_ref[g]
            mix_ref[rs, cs] = (u_ref[rs, cs].astype(F32) * sp).astype(BF16)
    y_a = _dot(mix_ref[...], wa_ref[...])

    hd = d // MEM_HEADS
    scale = hd ** -0.5
    for h in range(MEM_HEADS):
        cs = slice(h * hd, (h + 1) * hd)
        kh = mk_ref[:, cs].astype(BF16)
        vh = mv_ref[:, cs].astype(BF16)
        s = lax.dot_general(q_ref[:, cs], kh, (((1,), (1,)), ((), ())), preferred_element_type=F32) * scale
        p = jnp.exp(s - jnp.max(s, axis=-1, keepdims=True))
        p = p / jnp.sum(p, axis=-1, keepdims=True)
        mix_ref[:, cs] = _dot(p.astype(BF16), vh).astype(BF16)
    y_c = _dot(mix_ref[...], wc_ref[...])

    merged = (gates_ref[:, 0:d].astype(F32) * y_a + gates_ref[:, d:2 * d].astype(F32) * y_b
              + gates_ref[:, 2 * d:3 * d].astype(F32) * y_c)
    o_ref[...] = x_ref[...] + _dot(merged.astype(BF16), wo_ref[...])


def _mixer(x, u, vn, glu, hist, q, gates, mem_k, mem_v, ws, bs, w_a_out, conv_w, conv_b, conv_ln_g, conv_ln_b,
           w_b_out, w_c_out, w_o, tq, n_valid):
    b, t, d = x.shape
    m = mem_k.shape[1]
    tile = lambda w: pl.BlockSpec((None, tq, w), lambda bi, i: (bi, i, 0))
    per_b = lambda r, w: pl.BlockSpec((None, r, w), lambda bi, i: (bi, 0, 0))
    halo = pl.BlockSpec((None, HALO, d), lambda bi, i: (bi, jnp.maximum(i * (tq // HALO) - 1, 0), 0))
    return pl.pallas_call(
        functools.partial(_mixer_kernel, n_valid=n_valid),
        grid=(b, t // tq),
        in_specs=[tile(d), tile(d), tile(d), tile(d), halo, per_b(HALO, d), tile(d), tile(3 * d),
                  per_b(m, d), per_b(m, d),
                  _full(ws.shape), _full(bs.shape), _full(w_a_out.shape), _full(conv_w.shape), _full((1, d)),
                  _full((1, d)), _full((1, d)), _full(w_b_out.shape), _full(w_c_out.shape), _full(w_o.shape)],
        out_specs=tile(d),
        out_shape=jax.ShapeDtypeStruct((b, t, d), F32),
        scratch_shapes=[pltpu.VMEM((HALO + tq, d), F32), pltpu.VMEM((tq, d), BF16)],
        compiler_params=_cparams(("parallel", "arbitrary")), name="mixer",
    )(x, u, vn, glu, glu, hist, q, gates, mem_k, mem_v, ws, bs, w_a_out, conv_w, conv_b, conv_ln_g, conv_ln_b,
      w_b_out, w_c_out, w_o)


def _top_values(s, out_ref):
    cur = s
    for k in range(PEER_TOPK):
        m = jnp.max(cur, axis=0, keepdims=True)
        out_ref[k:k + 1, :] = m
        cur = jnp.where(cur == m, -jnp.inf, cur)


def _candidates(va_ref, vb_ref, op, fill):
    half = PEER_TOPK // 2
    rows = lax.broadcasted_iota(jnp.int32, (half, va_ref.shape[1]), 0)
    pieces = [op(va_ref[0:1, :], vb_ref[...])]
    for a in range(1, half):
        nb = PEER_TOPK // (a + 1)
        pieces.append(jnp.where(rows < nb, op(va_ref[a:a + 1, :], vb_ref[0:half, :]), fill))
    pieces.append(op(va_ref[half:PEER_TOPK, :], vb_ref[0:1, :]))
    return pieces


def _route_kernel(x_ref, g_ref, wqt_ref, k1_ref, k2_ref, hnt_ref, s2_ref, e2_ref, e1n_ref, tau2_ref,
                  v1_ref, v2_ref, ev1_ref, ev2_ref):
    hn_t = _rms(x_ref[...], g_ref[...]).T.astype(BF16)
    hnt_ref[...] = hn_t
    q_t = _dot(wqt_ref[...], hn_t)
    nk = PEER_NKEYS
    neg = -jnp.inf
    for h in range(PEER_HEADS):
        s1 = _dot(k1_ref[...], q_t[2 * h * nk:(2 * h + 1) * nk, :].astype(BF16))
        s2 = _dot(k2_ref[...], q_t[(2 * h + 1) * nk:(2 * h + 2) * nk, :].astype(BF16))
        _top_values(s1, v1_ref)
        _top_values(s2, v2_ref)
        m1 = v1_ref[0:1, :]
        m2 = v2_ref[0:1, :]
        cur = _candidates(v1_ref, v2_ref, lambda a, b: a + b, neg)
        tau = None
        for k in range(PEER_TOPK):
            tau = functools.reduce(jnp.maximum, [jnp.max(p, axis=0, keepdims=True) for p in cur])
            if k + 1 < PEER_TOPK:
                cur = [jnp.where(p == tau, neg, p) for p in cur]
        ev1_ref[...] = jnp.exp(v1_ref[...] - m1)
        ev2_ref[...] = jnp.exp(v2_ref[...] - m2)
        sums = _candidates(v1_ref, v2_ref, lambda a, b: a + b, neg)
        prods = _candidates(ev1_ref, ev2_ref, lambda a, b: a * b, 0.0)
        z = functools.reduce(
            lambda a, b: a + b,
            [jnp.sum(jnp.where(sm >= tau, pr, 0.0), axis=0, keepdims=True) for sm, pr in zip(sums, prods)])
        tau2 = jnp.full(s1.shape, jnp.inf, F32)
        for b in range(PEER_TOPK):
            vb = v2_ref[b:b + 1, :]
            tau2 = jnp.minimum(tau2, jnp.where(s1 + vb >= tau, vb, jnp.inf))
        s2_ref[h] = s2
        e2_ref[h] = jnp.exp(s2 - m2)
        e1n_ref[h] = jnp.exp(s1 - m1) / z
        tau2_ref[h] = tau2


def _route(x1, g_ffn, wq_t, k1, k2, tt):
    n, d = x1.shape
    hk = PEER_HEADS
    nk = PEER_NKEYS
    sc = pl.BlockSpec((hk, nk, tt), lambda i: (0, 0, i))
    sc_shape = jax.ShapeDtypeStruct((hk, nk, n), F32)
    return pl.pallas_call(
        _route_kernel, grid=(n // tt,),
        in_specs=[pl.BlockSpec((tt, d), lambda i: (i, 0)), _full((1, d)), _full(wq_t.shape), _full(k1.shape),
                  _full(k2.shape)],
        out_specs=[pl.BlockSpec((d, tt), lambda i: (0, i)), sc, sc, sc, sc],
        out_shape=[jax.ShapeDtypeStruct((d, n), BF16), sc_shape, sc_shape, sc_shape, sc_shape],
        scratch_shapes=[pltpu.VMEM((PEER_TOPK, tt), F32)] * 4,
        compiler_params=_cparams(("parallel",)), name="peer_route",
    )(x1, g_ffn, wq_t, k1, k2)


def _experts_kernel(hnt_ref, s2_ref, e2_ref, e1n_ref, tau2_ref, u_ref, vt_ref, x1_ref, go_ref, y_ref,
                    acc_ref, a_ref, w_ref):
    c = pl.program_id(1)
    ec, tb = a_ref.shape
    nk = PEER_NKEYS
    slabs = ec // nk

    @pl.when(c == 0)
    def _():
        acc_ref[...] = jnp.zeros_like(acc_ref)

    a_ref[...] = _dot(u_ref[...], hnt_ref[...])
    i1s = pl.ds(pl.multiple_of(c * slabs, slabs), slabs)

    def lane_chunk(l, carry):
        ls = pl.ds(pl.multiple_of(l * 128, 128), 128)
        e1g = [e1n_ref[h, i1s, ls] for h in range(PEER_HEADS)]
        t2g = [tau2_ref[h, i1s, ls] for h in range(PEER_HEADS)]
        for s in range(slabs):
            rs = slice(s * nk, (s + 1) * nk)
            gate = jnp.zeros((nk, 128), F32)
            for h in range(PEER_HEADS):
                sel = s2_ref[h, :, ls] >= t2g[h][s:s + 1, :]
                gate = gate + e1g[h][s:s + 1, :] * jnp.where(sel, e2_ref[h, :, ls], 0.0)
            w_ref[rs, ls] = (jax.nn.gelu(a_ref[rs, ls]) * gate).astype(BF16)
        return carry

    lax.fori_loop(0, tb // 128, lane_chunk, 0)
    acc_ref[...] += _dot(vt_ref[...], w_ref[...])

    @pl.when(c == pl.num_programs(1) - 1)
    def _():
        x2 = x1_ref[...] + acc_ref[...].T
        y_ref[...] = _rms(x2, go_ref[...])


def _experts(hn_t, s2, e2, e1n, tau2, u_bf, vt_bf, x1, g_out, tb, ec):
    n, d = x1.shape
    ne = u_bf.shape[0]
    assert (ec // PEER_NKEYS) % 8 == 0, "first-key rows of a chunk must form whole sublane groups"
    sc = pl.BlockSpec((PEER_HEADS, PEER_NKEYS, tb), lambda i, c: (0, 0, i))
    return pl.pallas_call(
        _experts_kernel, grid=(n // tb, ne // ec),
        in_specs=[pl.BlockSpec((d, tb), lambda i, c: (0, i)), sc, sc, sc, sc,
                  pl.BlockSpec((ec, d), lambda i, c: (c, 0)), pl.BlockSpec((d, ec), lambda i, c: (0, c)),
                  pl.BlockSpec((tb, d), lambda i, c: (i, 0)), _full((1, d))],
        out_specs=pl.BlockSpec((tb, d), lambda i, c: (i, 0)),
        out_shape=jax.ShapeDtypeStruct((n, d), F32),
        scratch_shapes=[pltpu.VMEM((d, tb), F32), pltpu.VMEM((ec, tb), F32), pltpu.VMEM((ec, tb), BF16)],
        compiler_params=_cparams(("parallel", "arbitrary")), name="peer_experts",
    )(hn_t, s2, e2, e1n, tau2, u_bf, vt_bf, x1, g_out)


def _tile_for(n, pref):
    t = pref
    while n % t:
        t //= 2
    return t


def _layer(x, hist, mem_k, mem_v, wts, t_valid):
    b, t, d = x.shape
    n = b * t
    xf = x.reshape(n, d)
    u, vn, glu, q, gates = _in_proj(xf, wts["g_mix"], wts["w_a"], wts["w_b"], wts["w_cg"], wts["b_gate"],
                                    wts["ln_g"], wts["ln_b"], _tile_for(n, 512))
    r3 = lambda a: a.reshape(b, t, a.shape[-1])
    x1 = _mixer(x, r3(u), r3(vn), r3(glu), hist, r3(q), r3(gates), mem_k, mem_v, wts["ws"], wts["bs"],
                wts["w_a_out"], wts["conv_w"], wts["conv_b"], wts["conv_ln_g"], wts["conv_ln_b"], wts["w_b_out"],
                wts["w_c_out"], wts["w_o"], _tile_for(t, 256), min(t_valid, GMLP_CHUNK))
    x1f = x1[:, :t_valid].reshape(b * t_valid, d)
    nv = b * t_valid
    hn_t, s2, e2, e1n, tau2 = _route(x1f, wts["g_ffn"], wts["wq_t"], wts["k1"], wts["k2"], _tile_for(nv, 256))
    y = _experts(hn_t, s2, e2, e1n, tau2, wts["u"], wts["v_t"], x1f, wts["g_out"], _tile_for(nv, 512), 1024)
    return y.reshape(b, t_valid, d), r3(glu), r3(vn)


def kernel(x_prompt, mem_prompt, x_sample, cache_mem_k, cache_mem_v, state_conv, norm_mix_g, w_in, b_gate, gmlp_ln_g, gmlp_ln_b, gmlp_ws, gmlp_bs, w_a_out, conv_w, conv_b, conv_ln_g, conv_ln_b, w_b_out, norm_mem_g, w_mem_kv, w_c_out, w_o, norm_ffn_g, peer_wq, peer_k1, peer_k2, peer_u, peer_v, norm_out_g):
    assert w_in.shape[0] == 1, "single trunk layer"
    bp, tp, d = x_prompt.shape
    bs_, ts, _ = x_sample.shape
    m = mem_prompt.shape[1]
    hist_rows = CONV_K - 1
    assert tp % GMLP_CHUNK == 0 and ts <= CHUNK and ts <= hist_rows
    row = lambda a: a.reshape(1, -1)
    w_in_b = w_in[0].astype(BF16)
    wts = dict(
        g_mix=row(norm_mix_g[0]), w_a=w_in_b[:, :2 * d], w_b=w_in_b[:, 2 * d:4 * d], w_cg=w_in_b[:, 4 * d:],
        b_gate=row(b_gate[0]), ln_g=row(gmlp_ln_g[0]), ln_b=row(gmlp_ln_b[0]),
        ws=gmlp_ws[0], bs=gmlp_bs[0][:, :, None], w_a_out=w_a_out[0].astype(BF16), conv_w=conv_w[0],
        conv_b=row(conv_b[0]), conv_ln_g=row(conv_ln_g[0]), conv_ln_b=row(conv_ln_b[0]),
        w_b_out=w_b_out[0].astype(BF16), w_c_out=w_c_out[0].astype(BF16), w_o=w_o[0].astype(BF16),
        g_ffn=row(norm_ffn_g[0]), wq_t=peer_wq[0].T.astype(BF16), k1=peer_k1[0].astype(BF16),
        k2=peer_k2[0].astype(BF16), u=peer_u[0].astype(BF16), v_t=peer_v[0].T.astype(BF16), g_out=row(norm_out_g),
    )

    kv = _norm_matmul(mem_prompt.reshape(bp * m, d), row(norm_mem_g[0]), w_mem_kv[0].astype(BF16), _tile_for(bp * m, 512))
    kv = kv.reshape(bp, m, 2 * d)
    mk_p, mv_p = kv[:, :, :d], kv[:, :, d:]
    y_p, glu_p, _ = _layer(x_prompt, jnp.zeros((bp, HALO, d), F32), mk_p, mv_p, wts, tp)

    xs = jnp.pad(x_sample, ((0, 0), (0, GMLP_CHUNK - ts), (0, 0)))
    hist_s = jnp.pad(state_conv[0], ((0, 0), (HALO - hist_rows, 0), (0, 0)))
    y_s, glu_s, vn_s = _layer(xs, hist_s, cache_mem_k[0].reshape(bs_, m, d), cache_mem_v[0].reshape(bs_, m, d), wts, ts)

    hd = d // MEM_HEADS
    conv_s = jnp.concatenate([state_conv[0][:, ts:], glu_s[:, :ts]], axis=1)
    return (y_p, y_s, glu_p[:, tp - hist_rows:][None], mk_p.reshape(bp, m, MEM_HEADS, hd)[None],
            mv_p.reshape(bp, m, MEM_HEADS, hd)[None], conv_s[None], vn_s[:, :ts][None])
```

```python
import functools

import jax
import jax.numpy as jnp
from jax import lax
from jax.experimental import pallas as pl
from jax.experimental.pallas import tpu as pltpu

EPS = 1e-6
CHUNK = 64
GMLP_CHUNK = 128
GMLP_GROUPS = 4
CONV_K = 31
MEM_HEADS = 4
PEER_HEADS = 8
PEER_NKEYS = 128
PEER_TOPK = 16
HALO = 32
GATE_ROWGROUPS = 4
EXPERT_PARTS = 2

F32 = jnp.float32
BF16 = jnp.bfloat16
VMEM_LIMIT = 48 * 1024 * 1024


def _cparams(sem):
    return pltpu.CompilerParams(dimension_semantics=sem, vmem_limit_bytes=VMEM_LIMIT)


def _rms(x, g):
    r = lax.rsqrt(jnp.mean(x * x, axis=-1, keepdims=True) + EPS)
    return (x * r) * g


def _layernorm(x, g, b):
    mu = jnp.mean(x, axis=-1, keepdims=True)
    d = x - mu
    var = jnp.mean(d * d, axis=-1, keepdims=True)
    return (d * lax.rsqrt(var + EPS)) * g + b


def _gelu_tanh(x):
    c = 0.7978845608028654
    inner = x * (c + (c * 0.044715) * (x * x))
    return (0.5 * x) * (1.0 + jnp.tanh(inner))


def _dot(a, b):
    return jnp.dot(a, b, preferred_element_type=F32)


def _full(shape):
    n = len(shape)
    return pl.BlockSpec(shape, lambda *_: (0,) * n)


def _norm_matmul_kernel(x_ref, g_ref, w_ref, o_ref):
    h = _rms(x_ref[...], g_ref[...]).astype(BF16)
    o_ref[...] = _dot(h, w_ref[...])


def _norm_matmul(x, g, w, tm):
    n, d = x.shape
    m = w.shape[1]
    return pl.pallas_call(
        _norm_matmul_kernel,
        grid=(n // tm,),
        in_specs=[pl.BlockSpec((tm, d), lambda i: (i, 0)), _full((1, d)), _full((d, m))],
        out_specs=pl.BlockSpec((tm, m), lambda i: (i, 0)),
        out_shape=jax.ShapeDtypeStruct((n, m), F32),
        compiler_params=_cparams(("parallel",)),
        name="mem_kv",
    )(x, g, w)


def _proj_a_kernel(x_ref, g_ref, w_ref, lg_ref, lb_ref, u_ref, vn_ref):
    h = _rms(x_ref[...], g_ref[...]).astype(BF16)
    d = u_ref.shape[1]
    u_ref[...] = _gelu_tanh(_dot(h, w_ref[:, :d])).astype(u_ref.dtype)
    v = _gelu_tanh(_dot(h, w_ref[:, d:]))
    vn_ref[...] = _layernorm(v, lg_ref[...], lb_ref[...])


def _proj_b_kernel(x_ref, g_ref, w_ref, glu_ref):
    h = _rms(x_ref[...], g_ref[...]).astype(BF16)
    d = glu_ref.shape[1]
    glu_ref[...] = _dot(h, w_ref[:, :d]) * jax.nn.sigmoid(_dot(h, w_ref[:, d:]))


def _proj_cg_kernel(x_ref, g_ref, w_ref, bg_ref, q_ref, gates_ref):
    h = _rms(x_ref[...], g_ref[...]).astype(BF16)
    d = q_ref.shape[1]
    q_ref[...] = _dot(h, w_ref[:, :d]).astype(q_ref.dtype)
    for k in range(gates_ref.shape[1] // d):
        z = _dot(h, w_ref[:, (k + 1) * d:(k + 2) * d]) + bg_ref[:, k * d:(k + 1) * d]
        gates_ref[:, k * d:(k + 1) * d] = jax.nn.sigmoid(z).astype(gates_ref.dtype)


def _in_proj(x, g_mix, w_a, w_b, w_cg, b_gate, ln_g, ln_b, tm):
    n, d = x.shape
    row = lambda w: pl.BlockSpec((tm, w), lambda i: (i, 0))
    grid = (n // tm,)
    u, vn = pl.pallas_call(
        _proj_a_kernel, grid=grid,
        in_specs=[row(d), _full((1, d)), _full(w_a.shape), _full((1, d)), _full((1, d))],
        out_specs=[row(d), row(d)],
        out_shape=[jax.ShapeDtypeStruct((n, d), BF16), jax.ShapeDtypeStruct((n, d), F32)],
        compiler_params=_cparams(("parallel",)), name="in_proj_gmlp",
    )(x, g_mix, w_a, ln_g, ln_b)
    glu = pl.pallas_call(
        _proj_b_kernel, grid=grid,
        in_specs=[row(d), _full((1, d)), _full(w_b.shape)],
        out_specs=row(d),
        out_shape=jax.ShapeDtypeStruct((n, d), F32),
        compiler_params=_cparams(("parallel",)), name="in_proj_conv",
    )(x, g_mix, w_b)
    ng = b_gate.shape[1]
    q, gates = pl.pallas_call(
        _proj_cg_kernel, grid=grid,
        in_specs=[row(d), _full((1, d)), _full(w_cg.shape), _full((1, ng))],
        out_specs=[row(d), row(ng)],
        out_shape=[jax.ShapeDtypeStruct((n, d), BF16), jax.ShapeDtypeStruct((n, ng), BF16)],
        compiler_params=_cparams(("parallel",)), name="in_proj_attn_gates",
    )(x, g_mix, w_cg, b_gate)
    return u, vn, glu, q, gates


def _mixer_kernel(x_ref, u_ref, vn_ref, glu_ref, halo_ref, hist_ref, q_ref, gates_ref, mk_ref, mv_ref,
                  ws_ref, bs_ref, wa_ref, cw_ref, cb_ref, clg_ref, clb_ref, wb_ref, wc_ref, wo_ref,
                  o_ref, ext_ref, mix_ref, *, n_valid):
    tq, d = x_ref.shape
    i = pl.program_id(1)

    @pl.when(i == 0)
    def _():
        ext_ref[0:HALO, :] = hist_ref[...]

    @pl.when(i > 0)
    def _():
        ext_ref[0:HALO, :] = halo_ref[...]

    ext_ref[HALO:HALO + tq, :] = glu_ref[...]
    off = HALO - (CONV_K - 1)
    c = cb_ref[...] + cw_ref[0:1, :] * ext_ref[off:off + tq, :]
    for k in range(1, CONV_K):
        c = c + cw_ref[k:k + 1, :] * ext_ref[off + k:off + k + tq, :]
    yb_in = jax.nn.silu(_layernorm(c, clg_ref[...], clb_ref[...])).astype(BF16)
    y_b = _dot(yb_in, wb_ref[...])

    ii = lax.broadcasted_iota(jnp.int32, (GMLP_CHUNK, GMLP_CHUNK), 0)
    jj = lax.broadcasted_iota(jnp.int32, (GMLP_CHUNK, GMLP_CHUNK), 1)
    shift = CHUNK.bit_length() - 1
    mask = ((jj >> shift) <= (ii >> shift)) & (ii < n_valid) & (jj < n_valid)
    gd = d // GMLP_GROUPS
    for g in range(GMLP_GROUPS):
        wg = jnp.where(mask, ws_ref[g], 0.0).astype(BF16)
        for cgrp in range(tq // GMLP_CHUNK):
            rs = slice(cgrp * GMLP_CHUNK, (cgrp + 1) * GMLP_CHUNK)
            cs = slice(g * gd, (g + 1) * gd)
            sp = _dot(wg, vn_ref[rs, cs].astype(BF16)) + bs_ref[g]
            mix_ref[rs, cs] = (u_ref[rs, cs].astype(F32) * sp).astype(BF16)
    y_a = _dot(mix_ref[...], wa_ref[...])

    hd = d // MEM_HEADS
    scale = hd ** -0.5
    for h in range(MEM_HEADS):
        cs = slice(h * hd, (h + 1) * hd)
        kh = mk_ref[:, cs].astype(BF16)
        vh = mv_ref[:, cs].astype(BF16)
        s = lax.dot_general(q_ref[:, cs], kh, (((1,), (1,)), ((), ())), preferred_element_type=F32) * scale
        p = jnp.exp(s - jnp.max(s, axis=-1, keepdims=True))
        p = p / jnp.sum(p, axis=-1, keepdims=True)
        mix_ref[:, cs] = _dot(p.astype(BF16), vh).astype(BF16)
    y_c = _dot(mix_ref[...], wc_ref[...])

    merged = (gates_ref[:, 0:d].astype(F32) * y_a + gates_ref[:, d:2 * d].astype(F32) * y_b
              + gates_ref[:, 2 * d:3 * d].astype(F32) * y_c)
    o_ref[...] = x_ref[...] + _dot(merged.astype(BF16), wo_ref[...])


def _mixer(x, u, vn, glu, hist, q, gates, mem_k, mem_v, ws, bs, w_a_out, conv_w, conv_b, conv_ln_g, conv_ln_b,
           w_b_out, w_c_out, w_o, tq, n_valid):
    b, t, d = x.shape
    m = mem_k.shape[1]
    tile = lambda w: pl.BlockSpec((None, tq, w), lambda bi, i: (bi, i, 0))
    per_b = lambda r, w: pl.BlockSpec((None, r, w), lambda bi, i: (bi, 0, 0))
    halo = pl.BlockSpec((None, HALO, d), lambda bi, i: (bi, jnp.maximum(i * (tq // HALO) - 1, 0), 0))
    return pl.pallas_call(
        functools.partial(_mixer_kernel, n_valid=n_valid),
        grid=(b, t // tq),
        in_specs=[tile(d), tile(d), tile(d), tile(d), halo, per_b(HALO, d), tile(d), tile(3 * d),
                  per_b(m, d), per_b(m, d),
                  _full(ws.shape), _full(bs.shape), _full(w_a_out.shape), _full(conv_w.shape), _full((1, d)),
                  _full((1, d)), _full((1, d)), _full(w_b_out.shape), _full(w_c_out.shape), _full(w_o.shape)],
        out_specs=tile(d),
        out_shape=jax.ShapeDtypeStruct((b, t, d), F32),
        scratch_shapes=[pltpu.VMEM((HALO + tq, d), F32), pltpu.VMEM((tq, d), BF16)],
        compiler_params=_cparams(("parallel", "arbitrary")), name="mixer",
    )(x, u, vn, glu, glu, hist, q, gates, mem_k, mem_v, ws, bs, w_a_out, conv_w, conv_b, conv_ln_g, conv_ln_b,
      w_b_out, w_c_out, w_o)


def _top_values(s, out_ref):
    cur = s
    for k in range(PEER_TOPK):
        m = jnp.max(cur, axis=0, keepdims=True)
        out_ref[k:k + 1, :] = m
        cur = jnp.where(cur == m, -jnp.inf, cur)


def _candidates(va_ref, vb_ref, op, fill):
    half = PEER_TOPK // 2
    rows = lax.broadcasted_iota(jnp.int32, (half, va_ref.shape[1]), 0)
    pieces = [op(va_ref[0:1, :], vb_ref[...])]
    for a in range(1, half):
        nb = PEER_TOPK // (a + 1)
        pieces.append(jnp.where(rows < nb, op(va_ref[a:a + 1, :], vb_ref[0:half, :]), fill))
    pieces.append(op(va_ref[half:PEER_TOPK, :], vb_ref[0:1, :]))
    return pieces


def _route_kernel(x_ref, g_ref, wqt_ref, k1_ref, k2_ref, hnt_ref, s2_ref, e2_ref, e1n_ref, tau2_ref,
                  v1_ref, v2_ref, ev1_ref, ev2_ref):
    hn_t = _rms(x_ref[...], g_ref[...]).T.astype(BF16)
    hnt_ref[...] = hn_t
    q_t = _dot(wqt_ref[...], hn_t)
    nk = PEER_NKEYS
    neg = -jnp.inf
    for h in range(PEER_HEADS):
        s1 = _dot(k1_ref[...], q_t[2 * h * nk:(2 * h + 1) * nk, :].astype(BF16))
        s2 = _dot(k2_ref[...], q_t[(2 * h + 1) * nk:(2 * h + 2) * nk, :].astype(BF16))
        _top_values(s1, v1_ref)
        _top_values(s2, v2_ref)
        m1 = v1_ref[0:1, :]
        m2 = v2_ref[0:1, :]
        cur = _candidates(v1_ref, v2_ref, lambda a, b: a + b, neg)
        tau = None
        for k in range(PEER_TOPK):
            tau = functools.reduce(jnp.maximum, [jnp.max(p, axis=0, keepdims=True) for p in cur])
            if k + 1 < PEER_TOPK:
                cur = [jnp.where(p == tau, neg, p) for p in cur]
        ev1_ref[...] = jnp.exp(v1_ref[...] - m1)
        ev2_ref[...] = jnp.exp(v2_ref[...] - m2)
        sums = _candidates(v1_ref, v2_ref, lambda a, b: a + b, neg)
        prods = _candidates(ev1_ref, ev2_ref, lambda a, b: a * b, 0.0)
        z = functools.reduce(
            lambda a, b: a + b,
            [jnp.sum(jnp.where(sm >= tau, pr, 0.0), axis=0, keepdims=True) for sm, pr in zip(sums, prods)])
        tau2 = jnp.full(s1.shape, jnp.inf, F32)
        for b in range(PEER_TOPK):
            vb = v2_ref[b:b + 1, :]
            tau2 = jnp.minimum(tau2, jnp.where(s1 + vb >= tau, vb, jnp.inf))
        e2 = jnp.exp(s2 - m2)
        for j in range(s2_ref.shape[0]):
            s2_ref[j, h] = s2[:, j * 128:(j + 1) * 128]
            e2_ref[j, h] = e2[:, j * 128:(j + 1) * 128]
        e1n_ref[h] = jnp.exp(s1 - m1) / z
        tau2_ref[h] = tau2


def _route(x1, g_ffn, wq_t, k1, k2, tt):
    n, d = x1.shape
    hk = PEER_HEADS
    nk = PEER_NKEYS
    sc = pl.BlockSpec((hk, nk, tt), lambda i: (0, 0, i))
    sc_shape = jax.ShapeDtypeStruct((hk, nk, n), F32)
    dense = pl.BlockSpec((tt // 128, hk, nk, 128), lambda i: (i, 0, 0, 0))
    dense_shape = jax.ShapeDtypeStruct((n // 128, hk, nk, 128), F32)
    return pl.pallas_call(
        _route_kernel, grid=(n // tt,),
        in_specs=[pl.BlockSpec((tt, d), lambda i: (i, 0)), _full((1, d)), _full(wq_t.shape), _full(k1.shape),
                  _full(k2.shape)],
        out_specs=[pl.BlockSpec((d, tt), lambda i: (0, i)), dense, dense, sc, sc],
        out_shape=[jax.ShapeDtypeStruct((d, n), BF16), dense_shape, dense_shape, sc_shape, sc_shape],
        scratch_shapes=[pltpu.VMEM((PEER_TOPK, tt), F32)] * 4,
        compiler_params=_cparams(("parallel",)), name="peer_route",
    )(x1, g_ffn, wq_t, k1, k2)


def _experts_kernel(hnt_ref, s2_ref, e2_ref, e1n_ref, tau2_ref, u_ref, vt_ref, x1_ref, go_ref, y_ref,
                    acc_ref, *part_refs):
    c = pl.program_id(1)
    nparts = len(part_refs) // 2
    a_refs, w_refs = part_refs[:nparts], part_refs[nparts:]
    hc, tb = a_refs[0].shape
    nk = PEER_NKEYS
    slabs = hc // nk
    rg = GATE_ROWGROUPS

    @pl.when(c == 0)
    def _():
        acc_ref[...] = jnp.zeros_like(acc_ref)

    def gated_activations(half, a_ref, w_ref):
        for l in range(tb // 128):
            ls = slice(l * 128, (l + 1) * 128)
            for rt in range(nk // (8 * rg)):
                rows = [slice((rt * rg + r) * 8, (rt * rg + r + 1) * 8) for r in range(rg)]
                gate = [[None] * rg for _ in range(slabs)]
                for h in range(PEER_HEADS):
                    s2v = [s2_ref[l, h, rows[r], :] for r in range(rg)]
                    e2v = [e2_ref[l, h, rows[r], :] for r in range(rg)]
                    for s in range(slabs):
                        k = half * slabs + s
                        t2b = jnp.broadcast_to(tau2_ref[h, k:k + 1, ls], (8, 128))
                        e1b = jnp.broadcast_to(e1n_ref[h, k:k + 1, ls], (8, 128))
                        for r in range(rg):
                            term = e1b * jnp.where(s2v[r] >= t2b, e2v[r], 0.0)
                            gate[s][r] = term if gate[s][r] is None else gate[s][r] + term
                for s in range(slabs):
                    for r in range(0, rg, 2):
                        er = slice(s * nk + rows[r].start, s * nk + rows[r + 1].stop)
                        g2 = jnp.concatenate([gate[s][r], gate[s][r + 1]], axis=0)
                        w_ref[er, ls] = (_gelu_tanh(a_ref[er, ls]) * g2).astype(BF16)

    for p in range(nparts):
        a_refs[p][...] = _dot(u_ref[p * hc:(p + 1) * hc, :], hnt_ref[...])
    for p in range(nparts):
        gated_activations(p, a_refs[p], w_refs[p])
        acc_ref[...] += _dot(vt_ref[:, p * hc:(p + 1) * hc], w_refs[p][...])

    @pl.when(c == pl.num_programs(1) - 1)
    def _():
        x2 = x1_ref[...] + acc_ref[...].T
        y_ref[...] = _rms(x2, go_ref[...])


def _experts(hn_t, s2, e2, e1n, tau2, u_bf, vt_bf, x1, g_out, tb):
    n, d = x1.shape
    ne = u_bf.shape[0]
    nk = PEER_NKEYS
    ec = 8 * nk
    hc = ec // EXPERT_PARTS
    dense = pl.BlockSpec((tb // 128, PEER_HEADS, nk, 128), lambda i, c: (i, 0, 0, 0))
    keyrow = pl.BlockSpec((PEER_HEADS, 8, tb), lambda i, c: (0, c, i))
    return pl.pallas_call(
        _experts_kernel, grid=(n // tb, ne // ec),
        in_specs=[pl.BlockSpec((d, tb), lambda i, c: (0, i)), dense, dense, keyrow, keyrow,
                  pl.BlockSpec((ec, d), lambda i, c: (c, 0)), pl.BlockSpec((d, ec), lambda i, c: (0, c)),
                  pl.BlockSpec((tb, d), lambda i, c: (i, 0)), _full((1, d))],
        out_specs=pl.BlockSpec((tb, d), lambda i, c: (i, 0)),
        out_shape=jax.ShapeDtypeStruct((n, d), F32),
        scratch_shapes=([pltpu.VMEM((d, tb), F32)] + [pltpu.VMEM((hc, tb), F32)] * EXPERT_PARTS
                        + [pltpu.VMEM((hc, tb), BF16)] * EXPERT_PARTS),
        compiler_params=_cparams(("parallel", "arbitrary")), name="peer_experts",
    )(hn_t, s2, e2, e1n, tau2, u_bf, vt_bf, x1, g_out)


def _tile_for(n, pref):
    t = pref
    while n % t:
        t //= 2
    return t


def _layer(x, hist, mem_k, mem_v, wts, t_valid):
    b, t, d = x.shape
    n = b * t
    xf = x.reshape(n, d)
    u, vn, glu, q, gates = _in_proj(xf, wts["g_mix"], wts["w_a"], wts["w_b"], wts["w_cg"], wts["b_gate"],
                                    wts["ln_g"], wts["ln_b"], _tile_for(n, 512))
    r3 = lambda a: a.reshape(b, t, a.shape[-1])
    x1 = _mixer(x, r3(u), r3(vn), r3(glu), hist, r3(q), r3(gates), mem_k, mem_v, wts["ws"], wts["bs"],
                wts["w_a_out"], wts["conv_w"], wts["conv_b"], wts["conv_ln_g"], wts["conv_ln_b"], wts["w_b_out"],
                wts["w_c_out"], wts["w_o"], _tile_for(t, 256), min(t_valid, GMLP_CHUNK))
    x1f = x1[:, :t_valid].reshape(b * t_valid, d)
    nv = b * t_valid
    hn_t, s2, e2, e1n, tau2 = _route(x1f, wts["g_ffn"], wts["wq_t"], wts["k1"], wts["k2"], _tile_for(nv, 256))
    y = _experts(hn_t, s2, e2, e1n, tau2, wts["u"], wts["v_t"], x1f, wts["g_out"], _tile_for(nv, 512))
    return y.reshape(b, t_valid, d), r3(glu), r3(vn)


def kernel(x_prompt, mem_prompt, x_sample, cache_mem_k, cache_mem_v, state_conv, norm_mix_g, w_in, b_gate, gmlp_ln_g, gmlp_ln_b, gmlp_ws, gmlp_bs, w_a_out, conv_w, conv_b, conv_ln_g, conv_ln_b, w_b_out, norm_mem_g, w_mem_kv, w_c_out, w_o, norm_ffn_g, peer_wq, peer_k1, peer_k2, peer_u, peer_v, norm_out_g):
    assert w_in.shape[0] == 1, "single trunk layer"
    bp, tp, d = x_prompt.shape
    bs_, ts, _ = x_sample.shape
    m = mem_prompt.shape[1]
    hist_rows = CONV_K - 1
    assert tp % GMLP_CHUNK == 0 and ts <= CHUNK and ts <= hist_rows
    row = lambda a: a.reshape(1, -1)
    w_in_b = w_in[0].astype(BF16)
    wts = dict(
        g_mix=row(norm_mix_g[0]), w_a=w_in_b[:, :2 * d], w_b=w_in_b[:, 2 * d:4 * d], w_cg=w_in_b[:, 4 * d:],
        b_gate=row(b_gate[0]), ln_g=row(gmlp_ln_g[0]), ln_b=row(gmlp_ln_b[0]),
        ws=gmlp_ws[0], bs=gmlp_bs[0][:, :, None], w_a_out=w_a_out[0].astype(BF16), conv_w=conv_w[0],
        conv_b=row(conv_b[0]), conv_ln_g=row(conv_ln_g[0]), conv_ln_b=row(conv_ln_b[0]),
        w_b_out=w_b_out[0].astype(BF16), w_c_out=w_c_out[0].astype(BF16), w_o=w_o[0].astype(BF16),
        g_ffn=row(norm_ffn_g[0]), wq_t=peer_wq[0].T.astype(BF16), k1=peer_k1[0].astype(BF16),
        k2=peer_k2[0].astype(BF16), u=peer_u[0].astype(BF16), v_t=peer_v[0].T.astype(BF16), g_out=row(norm_out_g),
    )

    kv = _norm_matmul(mem_prompt.reshape(bp * m, d), row(norm_mem_g[0]), w_mem_kv[0].astype(BF16), _tile_for(bp * m, 512))
    kv = kv.reshape(bp, m, 2 * d)
    mk_p, mv_p = kv[:, :, :d], kv[:, :, d:]
    y_p, glu_p, _ = _layer(x_prompt, jnp.zeros((bp, HALO, d), F32), mk_p, mv_p, wts, tp)

    xs = jnp.pad(x_sample, ((0, 0), (0, GMLP_CHUNK - ts), (0, 0)))
    hist_s = jnp.pad(state_conv[0], ((0, 0), (HALO - hist_rows, 0), (0, 0)))
    y_s, glu_s, vn_s = _layer(xs, hist_s, cache_mem_k[0].reshape(bs_, m, d), cache_mem_v[0].reshape(bs_, m, d), wts, ts)

    hd = d // MEM_HEADS
    conv_s = jnp.concatenate([state_conv[0][:, ts:], glu_s[:, :ts]], axis=1)
    return (y_p, y_s, glu_p[:, tp - hist_rows:][None], mk_p.reshape(bp, m, MEM_HEADS, hd)[None],
            mv_p.reshape(bp, m, MEM_HEADS, hd)[None], conv_s[None], vn_s[:, :ts][None])
```

```python
import functools

import jax
import jax.numpy as jnp
from jax import lax
from jax.experimental import pallas as pl
from jax.experimental.pallas import tpu as pltpu

EPS = 1e-6
CHUNK = 64
GMLP_CHUNK = 128
GMLP_GROUPS = 4
CONV_K = 31
MEM_HEADS = 4
PEER_HEADS = 8
PEER_NKEYS = 128
PEER_TOPK = 16
HALO = 32
GATE_ROWGROUPS = 4
EXPERT_PARTS = 2

F32 = jnp.float32
BF16 = jnp.bfloat16
VMEM_LIMIT = 48 * 1024 * 1024


def _cparams(sem):
    return pltpu.CompilerParams(dimension_semantics=sem, vmem_limit_bytes=VMEM_LIMIT)


def _rms(x, g):
    r = lax.rsqrt(jnp.mean(x * x, axis=-1, keepdims=True) + EPS)
    return (x * r) * g


def _layernorm(x, g, b):
    mu = jnp.mean(x, axis=-1, keepdims=True)
    d = x - mu
    var = jnp.mean(d * d, axis=-1, keepdims=True)
    return (d * lax.rsqrt(var + EPS)) * g + b


def _gelu_tanh(x):
    c = 0.7978845608028654
    inner = x * (c + (c * 0.044715) * (x * x))
    return (0.5 * x) * (1.0 + jnp.tanh(inner))


def _dot(a, b):
    return jnp.dot(a, b, preferred_element_type=F32)


def _full(shape):
    n = len(shape)
    return pl.BlockSpec(shape, lambda *_: (0,) * n)


def _norm_matmul_kernel(x_ref, g_ref, w_ref, o_ref):
    h = _rms(x_ref[...], g_ref[...]).astype(BF16)
    o_ref[...] = _dot(h, w_ref[...])


def _norm_matmul(x, g, w, tm):
    n, d = x.shape
    m = w.shape[1]
    return pl.pallas_call(
        _norm_matmul_kernel,
        grid=(n // tm,),
        in_specs=[pl.BlockSpec((tm, d), lambda i: (i, 0)), _full((1, d)), _full((d, m))],
        out_specs=pl.BlockSpec((tm, m), lambda i: (i, 0)),
        out_shape=jax.ShapeDtypeStruct((n, m), F32),
        compiler_params=_cparams(("parallel",)),
        name="mem_kv",
    )(x, g, w)


def _proj_a_kernel(x_ref, g_ref, w_ref, lg_ref, lb_ref, u_ref, vn_ref):
    h = _rms(x_ref[...], g_ref[...]).astype(BF16)
    d = u_ref.shape[1]
    u_ref[...] = _gelu_tanh(_dot(h, w_ref[:, :d])).astype(u_ref.dtype)
    v = _gelu_tanh(_dot(h, w_ref[:, d:]))
    vn_ref[...] = _layernorm(v, lg_ref[...], lb_ref[...])


def _proj_b_kernel(x_ref, g_ref, w_ref, glu_ref):
    h = _rms(x_ref[...], g_ref[...]).astype(BF16)
    d = glu_ref.shape[1]
    glu_ref[...] = _dot(h, w_ref[:, :d]) * jax.nn.sigmoid(_dot(h, w_ref[:, d:]))


def _proj_cg_kernel(x_ref, g_ref, w_ref, bg_ref, q_ref, gates_ref):
    h = _rms(x_ref[...], g_ref[...]).astype(BF16)
    d = q_ref.shape[1]
    q_ref[...] = _dot(h, w_ref[:, :d]).astype(q_ref.dtype)
    for k in range(gates_ref.shape[1] // d):
        z = _dot(h, w_ref[:, (k + 1) * d:(k + 2) * d]) + bg_ref[:, k * d:(k + 1) * d]
        gates_ref[:, k * d:(k + 1) * d] = jax.nn.sigmoid(z).astype(gates_ref.dtype)


def _in_proj(x, g_mix, w_a, w_b, w_cg, b_gate, ln_g, ln_b, tm):
    n, d = x.shape
    row = lambda w: pl.BlockSpec((tm, w), lambda i: (i, 0))
    grid = (n // tm,)
    u, vn = pl.pallas_call(
        _proj_a_kernel, grid=grid,
        in_specs=[row(d), _full((1, d)), _full(w_a.shape), _full((1, d)), _full((1, d))],
        out_specs=[row(d), row(d)],
        out_shape=[jax.ShapeDtypeStruct((n, d), BF16), jax.ShapeDtypeStruct((n, d), F32)],
        compiler_params=_cparams(("parallel",)), name="in_proj_gmlp",
    )(x, g_mix, w_a, ln_g, ln_b)
    glu = pl.pallas_call(
        _proj_b_kernel, grid=grid,
        in_specs=[row(d), _full((1, d)), _full(w_b.shape)],
        out_specs=row(d),
        out_shape=jax.ShapeDtypeStruct((n, d), F32),
        compiler_params=_cparams(("parallel",)), name="in_proj_conv",
    )(x, g_mix, w_b)
    ng = b_gate.shape[1]
    q, gates = pl.pallas_call(
        _proj_cg_kernel, grid=grid,
        in_specs=[row(d), _full((1, d)), _full(w_cg.shape), _full((1, ng))],
        out_specs=[row(d), row(ng)],
        out_shape=[jax.ShapeDtypeStruct((n, d), BF16), jax.ShapeDtypeStruct((n, ng), BF16)],
        compiler_params=_cparams(("parallel",)), name="in_proj_attn_gates",
    )(x, g_mix, w_cg, b_gate)
    return u, vn, glu, q, gates


def _mixer_kernel(x_ref, u_ref, vn_ref, glu_ref, halo_ref, hist_ref, q_ref, gates_ref, mk_ref, mv_ref,
                  ws_ref, bs_ref, wa_ref, cw_ref, cb_ref, clg_ref, clb_ref, wb_ref, wc_ref, wo_ref,
                  o_ref, ext_ref, mix_ref, *, n_valid):
    tq, d = x_ref.shape
    i = pl.program_id(1)

    @pl.when(i == 0)
    def _():
        ext_ref[0:HALO, :] = hist_ref[...]

    @pl.when(i > 0)
    def _():
        ext_ref[0:HALO, :] = halo_ref[...]

    ext_ref[HALO:HALO + tq, :] = glu_ref[...]
    off = HALO - (CONV_K - 1)
    c = cb_ref[...] + cw_ref[0:1, :] * ext_ref[off:off + tq, :]
    for k in range(1, CONV_K):
        c = c + cw_ref[k:k + 1, :] * ext_ref[off + k:off + k + tq, :]
    yb_in = jax.nn.silu(_layernorm(c, clg_ref[...], clb_ref[...])).astype(BF16)
    y_b = _dot(yb_in, wb_ref[...])

    ii = lax.broadcasted_iota(jnp.int32, (GMLP_CHUNK, GMLP_CHUNK), 0)
    jj = lax.broadcasted_iota(jnp.int32, (GMLP_CHUNK, GMLP_CHUNK), 1)
    shift = CHUNK.bit_length() - 1
    mask = ((jj >> shift) <= (ii >> shift)) & (ii < n_valid) & (jj < n_valid)
    gd = d // GMLP_GROUPS
    for g in range(GMLP_GROUPS):
        wg = jnp.where(mask, ws_ref[g], 0.0).astype(BF16)
        for cgrp in range(tq // GMLP_CHUNK):
            rs = slice(cgrp * GMLP_CHUNK, (cgrp + 1) * GMLP_CHUNK)
            cs = slice(g * gd, (g + 1) * gd)
            sp = _dot(wg, vn_ref[rs, cs].astype(BF16)) + bs_ref[g]
            mix_ref[rs, cs] = (u_ref[rs, cs].astype(F32) * sp).astype(BF16)
    y_a = _dot(mix_ref[...], wa_ref[...])

    hd = d // MEM_HEADS
    scale = hd ** -0.5
    for h in range(MEM_HEADS):
        cs = slice(h * hd, (h + 1) * hd)
        kh = mk_ref[:, cs].astype(BF16)
        vh = mv_ref[:, cs].astype(BF16)
        s = lax.dot_general(q_ref[:, cs], kh, (((1,), (1,)), ((), ())), preferred_element_type=F32) * scale
        p = jnp.exp(s - jnp.max(s, axis=-1, keepdims=True))
        p = p / jnp.sum(p, axis=-1, keepdims=True)
        mix_ref[:, cs] = _dot(p.astype(BF16), vh).astype(BF16)
    y_c = _dot(mix_ref[...], wc_ref[...])

    merged = (gates_ref[:, 0:d].astype(F32) * y_a + gates_ref[:, d:2 * d].astype(F32) * y_b
              + gates_ref[:, 2 * d:3 * d].astype(F32) * y_c)
    o_ref[...] = x_ref[...] + _dot(merged.astype(BF16), wo_ref[...])


def _mixer(x, u, vn, glu, hist, q, gates, mem_k, mem_v, ws, bs, w_a_out, conv_w, conv_b, conv_ln_g, conv_ln_b,
           w_b_out, w_c_out, w_o, tq, n_valid):
    b, t, d = x.shape
    m = mem_k.shape[1]
    tile = lambda w: pl.BlockSpec((None, tq, w), lambda bi, i: (bi, i, 0))
    per_b = lambda r, w: pl.BlockSpec((None, r, w), lambda bi, i: (bi, 0, 0))
    halo = pl.BlockSpec((None, HALO, d), lambda bi, i: (bi, jnp.maximum(i * (tq // HALO) - 1, 0), 0))
    return pl.pallas_call(
        functools.partial(_mixer_kernel, n_valid=n_valid),
        grid=(b, t // tq),
        in_specs=[tile(d), tile(d), tile(d), tile(d), halo, per_b(HALO, d), tile(d), tile(3 * d),
                  per_b(m, d), per_b(m, d),
                  _full(ws.shape), _full(bs.shape), _full(w_a_out.shape), _full(conv_w.shape), _full((1, d)),
                  _full((1, d)), _full((1, d)), _full(w_b_out.shape), _full(w_c_out.shape), _full(w_o.shape)],
        out_specs=tile(d),
        out_shape=jax.ShapeDtypeStruct((b, t, d), F32),
        scratch_shapes=[pltpu.VMEM((HALO + tq, d), F32), pltpu.VMEM((tq, d), BF16)],
        compiler_params=_cparams(("parallel", "arbitrary")), name="mixer",
    )(x, u, vn, glu, glu, hist, q, gates, mem_k, mem_v, ws, bs, w_a_out, conv_w, conv_b, conv_ln_g, conv_ln_b,
      w_b_out, w_c_out, w_o)


def _top_values(s, out_ref):
    cur = s
    for k in range(PEER_TOPK):
        m = jnp.max(cur, axis=0, keepdims=True)
        out_ref[k:k + 1, :] = m
        cur = jnp.where(cur == m, -jnp.inf, cur)


def _candidates(va_ref, vb_ref, op, fill):
    half = PEER_TOPK // 2
    rows = lax.broadcasted_iota(jnp.int32, (half, va_ref.shape[1]), 0)
    pieces = [op(va_ref[0:1, :], vb_ref[...])]
    for a in range(1, half):
        nb = PEER_TOPK // (a + 1)
        pieces.append(jnp.where(rows < nb, op(va_ref[a:a + 1, :], vb_ref[0:half, :]), fill))
    pieces.append(op(va_ref[half:PEER_TOPK, :], vb_ref[0:1, :]))
    return pieces


def _col_max(pieces):
    tiles = []
    for p in pieces:
        tiles.extend(p[r:r + 8, :] for r in range(0, p.shape[0], 8))
    return jnp.max(functools.reduce(jnp.maximum, tiles), axis=0, keepdims=True)


def _route_kernel(x_ref, g_ref, wqt_ref, k1_ref, k2_ref, hnt_ref, sd_ref, e1n_ref, et2_ref,
                  v1_ref, v2_ref, ev1_ref, ev2_ref):
    hn_t = _rms(x_ref[...], g_ref[...]).T.astype(BF16)
    hnt_ref[...] = hn_t
    q_t = _dot(wqt_ref[...], hn_t)
    nk = PEER_NKEYS
    neg = -jnp.inf
    for h in range(PEER_HEADS):
        s1 = _dot(k1_ref[...], q_t[2 * h * nk:(2 * h + 1) * nk, :].astype(BF16))
        s2 = _dot(k2_ref[...], q_t[(2 * h + 1) * nk:(2 * h + 2) * nk, :].astype(BF16))
        _top_values(s1, v1_ref)
        _top_values(s2, v2_ref)
        m1 = v1_ref[0:1, :]
        m2 = v2_ref[0:1, :]
        cur = _candidates(v1_ref, v2_ref, lambda a, b: a + b, neg)
        tau = None
        for k in range(PEER_TOPK):
            tau = _col_max(cur)
            if k + 1 < PEER_TOPK:
                cur = [jnp.where(p == tau, neg, p) for p in cur]
        ev1_ref[...] = jnp.exp(v1_ref[...] - m1)
        ev2_ref[...] = jnp.exp(v2_ref[...] - m2)
        sums = _candidates(v1_ref, v2_ref, lambda a, b: a + b, neg)
        prods = _candidates(ev1_ref, ev2_ref, lambda a, b: a * b, 0.0)
        z = functools.reduce(
            lambda a, b: a + b,
            [jnp.sum(jnp.where(sm >= tau, pr, 0.0), axis=0, keepdims=True) for sm, pr in zip(sums, prods)])
        et2 = jnp.full(s1.shape, jnp.inf, F32)
        for b in range(PEER_TOPK):
            et2 = jnp.minimum(et2, jnp.where(s1 + v2_ref[b:b + 1, :] >= tau, ev2_ref[b:b + 1, :], jnp.inf))
        sd = s2 - m2
        for j in range(sd_ref.shape[0]):
            sd_ref[j, h] = sd[:, j * 128:(j + 1) * 128]
        e1n_ref[h] = jnp.exp(s1 - m1) / z
        et2_ref[h] = et2


def _route(x1, g_ffn, wq_t, k1, k2, tt):
    n, d = x1.shape
    hk = PEER_HEADS
    nk = PEER_NKEYS
    sc = pl.BlockSpec((hk, nk, tt), lambda i: (0, 0, i))
    sc_shape = jax.ShapeDtypeStruct((hk, nk, n), F32)
    dense = pl.BlockSpec((tt // 128, hk, nk, 128), lambda i: (i, 0, 0, 0))
    dense_shape = jax.ShapeDtypeStruct((n // 128, hk, nk, 128), F32)
    return pl.pallas_call(
        _route_kernel, grid=(n // tt,),
        in_specs=[pl.BlockSpec((tt, d), lambda i: (i, 0)), _full((1, d)), _full(wq_t.shape), _full(k1.shape),
                  _full(k2.shape)],
        out_specs=[pl.BlockSpec((d, tt), lambda i: (0, i)), dense, sc, sc],
        out_shape=[jax.ShapeDtypeStruct((d, n), BF16), dense_shape, sc_shape, sc_shape],
        scratch_shapes=[pltpu.VMEM((PEER_TOPK, tt), F32)] * 4,
        compiler_params=_cparams(("parallel",)), name="peer_route",
    )(x1, g_ffn, wq_t, k1, k2)


def _experts_kernel(hnt_ref, s2_ref, e1n_ref, et2_ref, u_ref, vt_ref, x1_ref, go_ref, y_ref,
                    acc_ref, *part_refs):
    c = pl.program_id(1)
    nparts = len(part_refs) // 2
    a_refs, w_refs = part_refs[:nparts], part_refs[nparts:]
    hc, tb = a_refs[0].shape
    nk = PEER_NKEYS
    slabs = hc // nk
    rg = GATE_ROWGROUPS

    @pl.when(c == 0)
    def _():
        acc_ref[...] = jnp.zeros_like(acc_ref)

    def gated_activations(half, a_ref, w_ref):
        for l in range(tb // 128):
            ls = slice(l * 128, (l + 1) * 128)
            e1g = [e1n_ref[h, :, ls] for h in range(PEER_HEADS)]
            t2g = [et2_ref[h, :, ls] for h in range(PEER_HEADS)]
            for rt in range(nk // (8 * rg)):
                rows = [slice((rt * rg + r) * 8, (rt * rg + r + 1) * 8) for r in range(rg)]
                gate = [[None] * rg for _ in range(slabs)]
                for h in range(PEER_HEADS):
                    e2v = [jnp.exp(s2_ref[l, h, rows[r], :]) for r in range(rg)]
                    for s in range(slabs):
                        k = half * slabs + s
                        t2b = jnp.broadcast_to(t2g[h][k:k + 1, :], (8, 128))
                        e1b = jnp.broadcast_to(e1g[h][k:k + 1, :], (8, 128))
                        for r in range(rg):
                            term = e1b * jnp.where(e2v[r] >= t2b, e2v[r], 0.0)
                            gate[s][r] = term if gate[s][r] is None else gate[s][r] + term
                for s in range(slabs):
                    for r in range(0, rg, 2):
                        er = slice(s * nk + rows[r].start, s * nk + rows[r + 1].stop)
                        g2 = jnp.concatenate([gate[s][r], gate[s][r + 1]], axis=0)
                        w_ref[er, ls] = (_gelu_tanh(a_ref[er, ls]) * g2).astype(BF16)

    for p in range(nparts):
        a_refs[p][...] = _dot(u_ref[p * hc:(p + 1) * hc, :], hnt_ref[...])
    for p in range(nparts):
        gated_activations(p, a_refs[p], w_refs[p])
        acc_ref[...] += _dot(vt_ref[:, p * hc:(p + 1) * hc], w_refs[p][...])

    @pl.when(c == pl.num_programs(1) - 1)
    def _():
        x2 = x1_ref[...] + acc_ref[...].T
        y_ref[...] = _rms(x2, go_ref[...])


def _experts(hn_t, s2, e1n, et2, u_bf, vt_bf, x1, g_out, tb):
    n, d = x1.shape
    ne = u_bf.shape[0]
    nk = PEER_NKEYS
    ec = 8 * nk
    hc = ec // EXPERT_PARTS
    dense = pl.BlockSpec((tb // 128, PEER_HEADS, nk, 128), lambda i, c: (i, 0, 0, 0))
    keyrow = pl.BlockSpec((PEER_HEADS, 8, tb), lambda i, c: (0, c, i))
    return pl.pallas_call(
        _experts_kernel, grid=(n // tb, ne // ec),
        in_specs=[pl.BlockSpec((d, tb), lambda i, c: (0, i)), dense, keyrow, keyrow,
                  pl.BlockSpec((ec, d), lambda i, c: (c, 0)), pl.BlockSpec((None, d, ec), lambda i, c: (c, 0, 0)),
                  pl.BlockSpec((tb, d), lambda i, c: (i, 0)), _full((1, d))],
        out_specs=pl.BlockSpec((tb, d), lambda i, c: (i, 0)),
        out_shape=jax.ShapeDtypeStruct((n, d), F32),
        scratch_shapes=([pltpu.VMEM((d, tb), F32)] + [pltpu.VMEM((hc, tb), F32)] * EXPERT_PARTS
                        + [pltpu.VMEM((hc, tb), BF16)] * EXPERT_PARTS),
        compiler_params=_cparams(("parallel", "arbitrary")), name="peer_experts",
    )(hn_t, s2, e1n, et2, u_bf, vt_bf, x1, g_out)


def _tile_for(n, pref):
    t = pref
    while n % t:
        t //= 2
    return t


def _layer(x, hist, mem_k, mem_v, wts, t_valid):
    b, t, d = x.shape
    n = b * t
    xf = x.reshape(n, d)
    u, vn, glu, q, gates = _in_proj(xf, wts["g_mix"], wts["w_a"], wts["w_b"], wts["w_cg"], wts["b_gate"],
                                    wts["ln_g"], wts["ln_b"], _tile_for(n, 512))
    r3 = lambda a: a.reshape(b, t, a.shape[-1])
    x1 = _mixer(x, r3(u), r3(vn), r3(glu), hist, r3(q), r3(gates), mem_k, mem_v, wts["ws"], wts["bs"],
                wts["w_a_out"], wts["conv_w"], wts["conv_b"], wts["conv_ln_g"], wts["conv_ln_b"], wts["w_b_out"],
                wts["w_c_out"], wts["w_o"], _tile_for(t, 256), min(t_valid, GMLP_CHUNK))
    x1f = x1[:, :t_valid].reshape(b * t_valid, d)
    nv = b * t_valid
    hn_t, sd, e1n, et2 = _route(x1f, wts["g_ffn"], wts["wq_t"], wts["k1"], wts["k2"], _tile_for(nv, 256))
    y = _experts(hn_t, sd, e1n, et2, wts["u"], wts["v_t"], x1f, wts["g_out"], _tile_for(nv, 512))
    return y.reshape(b, t_valid, d), r3(glu), r3(vn)


def kernel(x_prompt, mem_prompt, x_sample, cache_mem_k, cache_mem_v, state_conv, norm_mix_g, w_in, b_gate, gmlp_ln_g, gmlp_ln_b, gmlp_ws, gmlp_bs, w_a_out, conv_w, conv_b, conv_ln_g, conv_ln_b, w_b_out, norm_mem_g, w_mem_kv, w_c_out, w_o, norm_ffn_g, peer_wq, peer_k1, peer_k2, peer_u, peer_v, norm_out_g):
    assert w_in.shape[0] == 1, "single trunk layer"
    bp, tp, d = x_prompt.shape
    bs_, ts, _ = x_sample.shape
    m = mem_prompt.shape[1]
    hist_rows = CONV_K - 1
    assert tp % GMLP_CHUNK == 0 and ts <= CHUNK and ts <= hist_rows
    row = lambda a: a.reshape(1, -1)
    w_in_b = w_in[0].astype(BF16)
    wts = dict(
        g_mix=row(norm_mix_g[0]), w_a=w_in_b[:, :2 * d], w_b=w_in_b[:, 2 * d:4 * d], w_cg=w_in_b[:, 4 * d:],
        b_gate=row(b_gate[0]), ln_g=row(gmlp_ln_g[0]), ln_b=row(gmlp_ln_b[0]),
        ws=gmlp_ws[0], bs=gmlp_bs[0][:, :, None], w_a_out=w_a_out[0].astype(BF16), conv_w=conv_w[0],
        conv_b=row(conv_b[0]), conv_ln_g=row(conv_ln_g[0]), conv_ln_b=row(conv_ln_b[0]),
        w_b_out=w_b_out[0].astype(BF16), w_c_out=w_c_out[0].astype(BF16), w_o=w_o[0].astype(BF16),
        g_ffn=row(norm_ffn_g[0]), wq_t=peer_wq[0].T.astype(BF16), k1=peer_k1[0].astype(BF16),
        k2=peer_k2[0].astype(BF16), u=peer_u[0].astype(BF16), g_out=row(norm_out_g),
        v_t=jnp.swapaxes(peer_v[0].reshape(-1, 8 * PEER_NKEYS, d), 1, 2).astype(BF16),
    )

    kv = _norm_matmul(mem_prompt.reshape(bp * m, d), row(norm_mem_g[0]), w_mem_kv[0].astype(BF16), _tile_for(bp * m, 512))
    kv = kv.reshape(bp, m, 2 * d)
    mk_p, mv_p = kv[:, :, :d], kv[:, :, d:]
    y_p, glu_p, _ = _layer(x_prompt, jnp.zeros((bp, HALO, d), F32), mk_p, mv_p, wts, tp)

    xs = jnp.pad(x_sample, ((0, 0), (0, GMLP_CHUNK - ts), (0, 0)))
    hist_s = jnp.pad(state_conv[0], ((0, 0), (HALO - hist_rows, 0), (0, 0)))
    y_s, glu_s, vn_s = _layer(xs, hist_s, cache_mem_k[0].reshape(bs_, m, d), cache_mem_v[0].reshape(bs_, m, d), wts, ts)

    hd = d // MEM_HEADS
    conv_s = jnp.concatenate([state_conv[0][:, ts:], glu_s[:, :ts]], axis=1)
    return (y_p, y_s, glu_p[:, tp - hist_rows:][None], mk_p.reshape(bp, m, MEM_HEADS, hd)[None],
            mv_p.reshape(bp, m, MEM_HEADS, hd)[None], conv_s[None], vn_s[:, :ts][None])
```

```python
import functools

import jax
import jax.numpy as jnp
from jax import lax
from jax.experimental import pallas as pl
from jax.experimental.pallas import tpu as pltpu

EPS = 1e-6
CHUNK = 64
GMLP_CHUNK = 128
GMLP_GROUPS = 4
CONV_K = 31
MEM_HEADS = 4
PEER_HEADS = 8
PEER_NKEYS = 128
PEER_TOPK = 16
HALO = 32
GATE_ROWGROUPS = 4
EXPERT_PARTS = 2

F32 = jnp.float32
BF16 = jnp.bfloat16
VMEM_LIMIT = 48 * 1024 * 1024


def _cparams(sem):
    return pltpu.CompilerParams(dimension_semantics=sem, vmem_limit_bytes=VMEM_LIMIT)


def _rms(x, g):
    r = lax.rsqrt(jnp.mean(x * x, axis=-1, keepdims=True) + EPS)
    return (x * r) * g


def _layernorm(x, g, b):
    mu = jnp.mean(x, axis=-1, keepdims=True)
    d = x - mu
    var = jnp.mean(d * d, axis=-1, keepdims=True)
    return (d * lax.rsqrt(var + EPS)) * g + b


def _gelu_tanh(x):
    c = 0.7978845608028654
    inner = x * (c + (c * 0.044715) * (x * x))
    return (0.5 * x) * (1.0 + jnp.tanh(inner))


def _dot(a, b):
    return jnp.dot(a, b, preferred_element_type=F32)


def _full(shape):
    n = len(shape)
    return pl.BlockSpec(shape, lambda *_: (0,) * n)


def _norm_matmul_kernel(x_ref, g_ref, w_ref, o_ref):
    h = _rms(x_ref[...], g_ref[...]).astype(BF16)
    o_ref[...] = _dot(h, w_ref[...])


def _norm_matmul(x, g, w, tm):
    n, d = x.shape
    m = w.shape[1]
    return pl.pallas_call(
        _norm_matmul_kernel,
        grid=(n // tm,),
        in_specs=[pl.BlockSpec((tm, d), lambda i: (i, 0)), _full((1, d)), _full((d, m))],
        out_specs=pl.BlockSpec((tm, m), lambda i: (i, 0)),
        out_shape=jax.ShapeDtypeStruct((n, m), F32),
        compiler_params=_cparams(("parallel",)),
        name="mem_kv",
    )(x, g, w)


def _proj_a_kernel(x_ref, g_ref, w_ref, lg_ref, lb_ref, u_ref, vn_ref):
    h = _rms(x_ref[...], g_ref[...]).astype(BF16)
    d = u_ref.shape[1]
    u_ref[...] = _gelu_tanh(_dot(h, w_ref[:, :d])).astype(u_ref.dtype)
    v = _gelu_tanh(_dot(h, w_ref[:, d:]))
    vn_ref[...] = _layernorm(v, lg_ref[...], lb_ref[...])


def _proj_b_kernel(x_ref, g_ref, w_ref, glu_ref):
    h = _rms(x_ref[...], g_ref[...]).astype(BF16)
    d = glu_ref.shape[1]
    glu_ref[...] = _dot(h, w_ref[:, :d]) * jax.nn.sigmoid(_dot(h, w_ref[:, d:]))


def _proj_cg_kernel(x_ref, g_ref, w_ref, bg_ref, q_ref, gates_ref):
    h = _rms(x_ref[...], g_ref[...]).astype(BF16)
    d = q_ref.shape[1]
    q_ref[...] = _dot(h, w_ref[:, :d]).astype(q_ref.dtype)
    for k in range(gates_ref.shape[1] // d):
        z = _dot(h, w_ref[:, (k + 1) * d:(k + 2) * d]) + bg_ref[:, k * d:(k + 1) * d]
        gates_ref[:, k * d:(k + 1) * d] = jax.nn.sigmoid(z).astype(gates_ref.dtype)


def _in_proj(x, g_mix, w_a, w_b, w_cg, b_gate, ln_g, ln_b, tm):
    n, d = x.shape
    row = lambda w: pl.BlockSpec((tm, w), lambda i: (i, 0))
    grid = (n // tm,)
    u, vn = pl.pallas_call(
        _proj_a_kernel, grid=grid,
        in_specs=[row(d), _full((1, d)), _full(w_a.shape), _full((1, d)), _full((1, d))],
        out_specs=[row(d), row(d)],
        out_shape=[jax.ShapeDtypeStruct((n, d), BF16), jax.ShapeDtypeStruct((n, d), F32)],
        compiler_params=_cparams(("parallel",)), name="in_proj_gmlp",
    )(x, g_mix, w_a, ln_g, ln_b)
    glu = pl.pallas_call(
        _proj_b_kernel, grid=grid,
        in_specs=[row(d), _full((1, d)), _full(w_b.shape)],
        out_specs=row(d),
        out_shape=jax.ShapeDtypeStruct((n, d), F32),
        compiler_params=_cparams(("parallel",)), name="in_proj_conv",
    )(x, g_mix, w_b)
    ng = b_gate.shape[1]
    q, gates = pl.pallas_call(
        _proj_cg_kernel, grid=grid,
        in_specs=[row(d), _full((1, d)), _full(w_cg.shape), _full((1, ng))],
        out_specs=[row(d), row(ng)],
        out_shape=[jax.ShapeDtypeStruct((n, d), BF16), jax.ShapeDtypeStruct((n, ng), BF16)],
        compiler_params=_cparams(("parallel",)), name="in_proj_attn_gates",
    )(x, g_mix, w_cg, b_gate)
    return u, vn, glu, q, gates


def _mixer_kernel(x_ref, u_ref, vn_ref, glu_ref, halo_ref, hist_ref, q_ref, gates_ref, mk_ref, mv_ref,
                  ws_ref, bs_ref, wa_ref, cw_ref, cb_ref, clg_ref, clb_ref, wb_ref, wc_ref, wo_ref,
                  o_ref, ext_ref, mix_ref, *, n_valid):
    tq, d = x_ref.shape
    i = pl.program_id(1)

    @pl.when(i == 0)
    def _():
        ext_ref[0:HALO, :] = hist_ref[...]

    @pl.when(i > 0)
    def _():
        ext_ref[0:HALO, :] = halo_ref[...]

    ext_ref[HALO:HALO + tq, :] = glu_ref[...]
    off = HALO - (CONV_K - 1)
    c = cb_ref[...] + cw_ref[0:1, :] * ext_ref[off:off + tq, :]
    for k in range(1, CONV_K):
        c = c + cw_ref[k:k + 1, :] * ext_ref[off + k:off + k + tq, :]
    yb_in = jax.nn.silu(_layernorm(c, clg_ref[...], clb_ref[...])).astype(BF16)
    y_b = _dot(yb_in, wb_ref[...])

    ii = lax.broadcasted_iota(jnp.int32, (GMLP_CHUNK, GMLP_CHUNK), 0)
    jj = lax.broadcasted_iota(jnp.int32, (GMLP_CHUNK, GMLP_CHUNK), 1)
    shift = CHUNK.bit_length() - 1
    mask = ((jj >> shift) <= (ii >> shift)) & (ii < n_valid) & (jj < n_valid)
    gd = d // GMLP_GROUPS
    for g in range(GMLP_GROUPS):
        wg = jnp.where(mask, ws_ref[g], 0.0).astype(BF16)
        for cgrp in range(tq // GMLP_CHUNK):
            rs = slice(cgrp * GMLP_CHUNK, (cgrp + 1) * GMLP_CHUNK)
            cs = slice(g * gd, (g + 1) * gd)
            sp = _dot(wg, vn_ref[rs, cs].astype(BF16)) + bs_ref[g]
            mix_ref[rs, cs] = (u_ref[rs, cs].astype(F32) * sp).astype(BF16)
    y_a = _dot(mix_ref[...], wa_ref[...])

    hd = d // MEM_HEADS
    scale = hd ** -0.5
    for h in range(MEM_HEADS):
        cs = slice(h * hd, (h + 1) * hd)
        kh = mk_ref[:, cs].astype(BF16)
        vh = mv_ref[:, cs].astype(BF16)
        s = lax.dot_general(q_ref[:, cs], kh, (((1,), (1,)), ((), ())), preferred_element_type=F32) * scale
        p = jnp.exp(s - jnp.max(s, axis=-1, keepdims=True))
        p = p / jnp.sum(p, axis=-1, keepdims=True)
        mix_ref[:, cs] = _dot(p.astype(BF16), vh).astype(BF16)
    y_c = _dot(mix_ref[...], wc_ref[...])

    merged = (gates_ref[:, 0:d].astype(F32) * y_a + gates_ref[:, d:2 * d].astype(F32) * y_b
              + gates_ref[:, 2 * d:3 * d].astype(F32) * y_c)
    o_ref[...] = x_ref[...] + _dot(merged.astype(BF16), wo_ref[...])


def _mixer(x, u, vn, glu, hist, q, gates, mem_k, mem_v, ws, bs, w_a_out, conv_w, conv_b, conv_ln_g, conv_ln_b,
           w_b_out, w_c_out, w_o, tq, n_valid):
    b, t, d = x.shape
    m = mem_k.shape[1]
    tile = lambda w: pl.BlockSpec((None, tq, w), lambda bi, i: (bi, i, 0))
    per_b = lambda r, w: pl.BlockSpec((None, r, w), lambda bi, i: (bi, 0, 0))
    halo = pl.BlockSpec((None, HALO, d), lambda bi, i: (bi, jnp.maximum(i * (tq // HALO) - 1, 0), 0))
    return pl.pallas_call(
        functools.partial(_mixer_kernel, n_valid=n_valid),
        grid=(b, t // tq),
        in_specs=[tile(d), tile(d), tile(d), tile(d), halo, per_b(HALO, d), tile(d), tile(3 * d),
                  per_b(m, d), per_b(m, d),
                  _full(ws.shape), _full(bs.shape), _full(w_a_out.shape), _full(conv_w.shape), _full((1, d)),
                  _full((1, d)), _full((1, d)), _full(w_b_out.shape), _full(w_c_out.shape), _full(w_o.shape)],
        out_specs=tile(d),
        out_shape=jax.ShapeDtypeStruct((b, t, d), F32),
        scratch_shapes=[pltpu.VMEM((HALO + tq, d), F32), pltpu.VMEM((tq, d), BF16)],
        compiler_params=_cparams(("parallel", "arbitrary")), name="mixer",
    )(x, u, vn, glu, glu, hist, q, gates, mem_k, mem_v, ws, bs, w_a_out, conv_w, conv_b, conv_ln_g, conv_ln_b,
      w_b_out, w_c_out, w_o)


def _sorting_network(n):
    out = []
    p = 1
    while p < n:
        k = p
        while k >= 1:
            for j in range(k % p, n - k, 2 * k):
                for i in range(min(k, n - j - k)):
                    if (i + j) // (2 * p) == (i + j + k) // (2 * p):
                        out.append((i + j, i + j + k))
            k //= 2
        p *= 2
    return out


def _top_values(s, out_ref):
    tiles = [s[r:r + 8, :] for r in range(0, s.shape[0], 8)]
    for i, j in _sorting_network(len(tiles)):
        tiles[i], tiles[j] = jnp.maximum(tiles[i], tiles[j]), jnp.minimum(tiles[i], tiles[j])
    depth = min(len(tiles), PEER_TOPK)
    tiles = tiles[:depth]
    for k in range(PEER_TOPK):
        m = jnp.max(tiles[0], axis=0, keepdims=True)
        out_ref[k:k + 1, :] = m
        hit = tiles[0] == m
        left = min(depth, PEER_TOPK - 1 - k)
        tiles = [jnp.where(hit, tiles[t + 1] if t + 1 < len(tiles) else -jnp.inf, tiles[t]) for t in range(left)]


def _candidates(va_ref, vb_ref, op, fill):
    half = PEER_TOPK // 2
    rows = lax.broadcasted_iota(jnp.int32, (half, va_ref.shape[1]), 0)
    pieces = [op(va_ref[0:1, :], vb_ref[...])]
    for a in range(1, half):
        nb = PEER_TOPK // (a + 1)
        pieces.append(jnp.where(rows < nb, op(va_ref[a:a + 1, :], vb_ref[0:half, :]), fill))
    pieces.append(op(va_ref[half:PEER_TOPK, :], vb_ref[0:1, :]))
    return pieces


def _col_max(pieces):
    tiles = []
    for p in pieces:
        tiles.extend(p[r:r + 8, :] for r in range(0, p.shape[0], 8))
    return jnp.max(functools.reduce(jnp.maximum, tiles), axis=0, keepdims=True)


def _route_kernel(x_ref, g_ref, wqt_ref, k1_ref, k2_ref, hnt_ref, sd_ref, e1n_ref, et2_ref,
                  v1_ref, v2_ref, ev1_ref, ev2_ref):
    hn_t = _rms(x_ref[...], g_ref[...]).T.astype(BF16)
    hnt_ref[...] = hn_t
    q_t = _dot(wqt_ref[...], hn_t)
    nk = PEER_NKEYS
    neg = -jnp.inf
    for h in range(PEER_HEADS):
        s1 = _dot(k1_ref[...], q_t[2 * h * nk:(2 * h + 1) * nk, :].astype(BF16))
        s2 = _dot(k2_ref[...], q_t[(2 * h + 1) * nk:(2 * h + 2) * nk, :].astype(BF16))
        _top_values(s1, v1_ref)
        _top_values(s2, v2_ref)
        m1 = v1_ref[0:1, :]
        m2 = v2_ref[0:1, :]
        cur = _candidates(v1_ref, v2_ref, lambda a, b: a + b, neg)
        tau = None
        for k in range(PEER_TOPK):
            tau = _col_max(cur)
            if k + 1 < PEER_TOPK:
                cur = [jnp.where(p == tau, neg, p) for p in cur]
        ev1_ref[...] = jnp.exp(v1_ref[...] - m1)
        ev2_ref[...] = jnp.exp(v2_ref[...] - m2)
        sums = _candidates(v1_ref, v2_ref, lambda a, b: a + b, neg)
        prods = _candidates(ev1_ref, ev2_ref, lambda a, b: a * b, 0.0)
        z = functools.reduce(
            lambda a, b: a + b,
            [jnp.sum(jnp.where(sm >= tau, pr, 0.0), axis=0, keepdims=True) for sm, pr in zip(sums, prods)])
        half = PEER_TOPK // 2
        ev2_pieces = ([ev2_ref[...]] + [ev2_ref[0:half, :]] * (half - 1)
                      + [jnp.broadcast_to(ev2_ref[0:1, :], (half, s1.shape[1]))])
        thr = [jnp.where(sm >= tau, ev, jnp.inf) for sm, ev in zip(sums, ev2_pieces)]
        et2_rank = [jnp.min(t, axis=0, keepdims=True) for t in thr[:half]] + [thr[half][a:a + 1, :] for a in range(half)]
        et2 = jnp.full(s1.shape, jnp.inf, F32)
        for a in reversed(range(PEER_TOPK)):
            et2 = jnp.where(s1 >= v1_ref[a:a + 1, :], et2_rank[a], et2)
        sd = s2 - m2
        for j in range(sd_ref.shape[0]):
            sd_ref[j, h] = sd[:, j * 128:(j + 1) * 128]
        e1n_ref[h] = jnp.exp(s1 - m1) / z
        et2_ref[h] = et2


def _route(x1, g_ffn, wq_t, k1, k2, tt):
    n, d = x1.shape
    hk = PEER_HEADS
    nk = PEER_NKEYS
    sc = pl.BlockSpec((hk, nk, tt), lambda i: (0, 0, i))
    sc_shape = jax.ShapeDtypeStruct((hk, nk, n), F32)
    dense = pl.BlockSpec((tt // 128, hk, nk, 128), lambda i: (i, 0, 0, 0))
    dense_shape = jax.ShapeDtypeStruct((n // 128, hk, nk, 128), F32)
    return pl.pallas_call(
        _route_kernel, grid=(n // tt,),
        in_specs=[pl.BlockSpec((tt, d), lambda i: (i, 0)), _full((1, d)), _full(wq_t.shape), _full(k1.shape),
                  _full(k2.shape)],
        out_specs=[pl.BlockSpec((d, tt), lambda i: (0, i)), dense, sc, sc],
        out_shape=[jax.ShapeDtypeStruct((d, n), BF16), dense_shape, sc_shape, sc_shape],
        scratch_shapes=[pltpu.VMEM((PEER_TOPK, tt), F32)] * 4,
        compiler_params=_cparams(("parallel",)), name="peer_route",
    )(x1, g_ffn, wq_t, k1, k2)


def _experts_kernel(hnt_ref, s2_ref, e1n_ref, et2_ref, u_ref, vt_ref, x1_ref, go_ref, y_ref,
                    acc_ref, *part_refs):
    c = pl.program_id(1)
    nparts = len(part_refs) // 2
    a_refs, w_refs = part_refs[:nparts], part_refs[nparts:]
    hc, tb = a_refs[0].shape
    nk = PEER_NKEYS
    slabs = hc // nk
    rg = GATE_ROWGROUPS

    @pl.when(c == 0)
    def _():
        acc_ref[...] = jnp.zeros_like(acc_ref)

    def gated_activations(half, a_ref, w_ref):
        for l in range(tb // 128):
            ls = slice(l * 128, (l + 1) * 128)
            e1g = [e1n_ref[h, :, ls] for h in range(PEER_HEADS)]
            t2g = [et2_ref[h, :, ls] for h in range(PEER_HEADS)]
            for rt in range(nk // (8 * rg)):
                rows = [slice((rt * rg + r) * 8, (rt * rg + r + 1) * 8) for r in range(rg)]
                gate = [[None] * rg for _ in range(slabs)]
                for h in range(PEER_HEADS):
                    e2v = [jnp.exp(s2_ref[l, h, rows[r], :]) for r in range(rg)]
                    for s in range(slabs):
                        k = half * slabs + s
                        t2b = jnp.broadcast_to(t2g[h][k:k + 1, :], (8, 128))
                        e1b = jnp.broadcast_to(e1g[h][k:k + 1, :], (8, 128))
                        for r in range(rg):
                            term = e1b * jnp.where(e2v[r] >= t2b, e2v[r], 0.0)
                            gate[s][r] = term if gate[s][r] is None else gate[s][r] + term
                for s in range(slabs):
                    for r in range(0, rg, 2):
                        er = slice(s * nk + rows[r].start, s * nk + rows[r + 1].stop)
                        g2 = jnp.concatenate([gate[s][r], gate[s][r + 1]], axis=0)
                        w_ref[er, ls] = (_gelu_tanh(a_ref[er, ls]) * g2).astype(BF16)

    for p in range(nparts):
        a_refs[p][...] = _dot(u_ref[p * hc:(p + 1) * hc, :], hnt_ref[...])
    for p in range(nparts):
        gated_activations(p, a_refs[p], w_refs[p])
        acc_ref[...] += _dot(vt_ref[:, p * hc:(p + 1) * hc], w_refs[p][...])

    @pl.when(c == pl.num_programs(1) - 1)
    def _():
        x2 = x1_ref[...] + acc_ref[...].T
        y_ref[...] = _rms(x2, go_ref[...])


def _experts(hn_t, s2, e1n, et2, u_bf, vt_bf, x1, g_out, tb):
    n, d = x1.shape
    ne = u_bf.shape[0]
    nk = PEER_NKEYS
    ec = 8 * nk
    hc = ec // EXPERT_PARTS
    dense = pl.BlockSpec((tb // 128, PEER_HEADS, nk, 128), lambda i, c: (i, 0, 0, 0))
    keyrow = pl.BlockSpec((PEER_HEADS, 8, tb), lambda i, c: (0, c, i))
    return pl.pallas_call(
        _experts_kernel, grid=(n // tb, ne // ec),
        in_specs=[pl.BlockSpec((d, tb), lambda i, c: (0, i)), dense, keyrow, keyrow,
                  pl.BlockSpec((ec, d), lambda i, c: (c, 0)), pl.BlockSpec((None, d, ec), lambda i, c: (c, 0, 0)),
                  pl.BlockSpec((tb, d), lambda i, c: (i, 0)), _full((1, d))],
        out_specs=pl.BlockSpec((tb, d), lambda i, c: (i, 0)),
        out_shape=jax.ShapeDtypeStruct((n, d), F32),
        scratch_shapes=([pltpu.VMEM((d, tb), F32)] + [pltpu.VMEM((hc, tb), F32)] * EXPERT_PARTS
                        + [pltpu.VMEM((hc, tb), BF16)] * EXPERT_PARTS),
        compiler_params=_cparams(("parallel", "arbitrary")), name="peer_experts",
    )(hn_t, s2, e1n, et2, u_bf, vt_bf, x1, g_out)


def _tile_for(n, pref):
    t = pref
    while n % t:
        t //= 2
    return t


def _layer(x, hist, mem_k, mem_v, wts, t_valid):
    b, t, d = x.shape
    n = b * t
    xf = x.reshape(n, d)
    u, vn, glu, q, gates = _in_proj(xf, wts["g_mix"], wts["w_a"], wts["w_b"], wts["w_cg"], wts["b_gate"],
                                    wts["ln_g"], wts["ln_b"], _tile_for(n, 512))
    r3 = lambda a: a.reshape(b, t, a.shape[-1])
    x1 = _mixer(x, r3(u), r3(vn), r3(glu), hist, r3(q), r3(gates), mem_k, mem_v, wts["ws"], wts["bs"],
                wts["w_a_out"], wts["conv_w"], wts["conv_b"], wts["conv_ln_g"], wts["conv_ln_b"], wts["w_b_out"],
                wts["w_c_out"], wts["w_o"], _tile_for(t, 256), min(t_valid, GMLP_CHUNK))
    x1f = x1[:, :t_valid].reshape(b * t_valid, d)
    nv = b * t_valid
    hn_t, sd, e1n, et2 = _route(x1f, wts["g_ffn"], wts["wq_t"], wts["k1"], wts["k2"], _tile_for(nv, 256))
    y = _experts(hn_t, sd, e1n, et2, wts["u"], wts["v_t"], x1f, wts["g_out"], _tile_for(nv, 512))
    return y.reshape(b, t_valid, d), r3(glu), r3(vn)


def kernel(x_prompt, mem_prompt, x_sample, cache_mem_k, cache_mem_v, state_conv, norm_mix_g, w_in, b_gate, gmlp_ln_g, gmlp_ln_b, gmlp_ws, gmlp_bs, w_a_out, conv_w, conv_b, conv_ln_g, conv_ln_b, w_b_out, norm_mem_g, w_mem_kv, w_c_out, w_o, norm_ffn_g, peer_wq, peer_k1, peer_k2, peer_u, peer_v, norm_out_g):
    assert w_in.shape[0] == 1, "single trunk layer"
    bp, tp, d = x_prompt.shape
    bs_, ts, _ = x_sample.shape
    m = mem_prompt.shape[1]
    hist_rows = CONV_K - 1
    assert tp % GMLP_CHUNK == 0 and ts <= CHUNK and ts <= hist_rows
    row = lambda a: a.reshape(1, -1)
    w_in_b = w_in[0].astype(BF16)
    wts = dict(
        g_mix=row(norm_mix_g[0]), w_a=w_in_b[:, :2 * d], w_b=w_in_b[:, 2 * d:4 * d], w_cg=w_in_b[:, 4 * d:],
        b_gate=row(b_gate[0]), ln_g=row(gmlp_ln_g[0]), ln_b=row(gmlp_ln_b[0]),
        ws=gmlp_ws[0], bs=gmlp_bs[0][:, :, None], w_a_out=w_a_out[0].astype(BF16), conv_w=conv_w[0],
        conv_b=row(conv_b[0]), conv_ln_g=row(conv_ln_g[0]), conv_ln_b=row(conv_ln_b[0]),
        w_b_out=w_b_out[0].astype(BF16), w_c_out=w_c_out[0].astype(BF16), w_o=w_o[0].astype(BF16),
        g_ffn=row(norm_ffn_g[0]), wq_t=peer_wq[0].T.astype(BF16), k1=peer_k1[0].astype(BF16),
        k2=peer_k2[0].astype(BF16), u=peer_u[0].astype(BF16), g_out=row(norm_out_g),
        v_t=jnp.swapaxes(peer_v[0].reshape(-1, 8 * PEER_NKEYS, d), 1, 2).astype(BF16),
    )

    kv = _norm_matmul(mem_prompt.reshape(bp * m, d), row(norm_mem_g[0]), w_mem_kv[0].astype(BF16), _tile_for(bp * m, 512))
    kv = kv.reshape(bp, m, 2 * d)
    mk_p, mv_p = kv[:, :, :d], kv[:, :, d:]
    y_p, glu_p, _ = _layer(x_prompt, jnp.zeros((bp, HALO, d), F32), mk_p, mv_p, wts, tp)

    xs = jnp.pad(x_sample, ((0, 0), (0, GMLP_CHUNK - ts), (0, 0)))
    hist_s = jnp.pad(state_conv[0], ((0, 0), (HALO - hist_rows, 0), (0, 0)))
    y_s, glu_s, vn_s = _layer(xs, hist_s, cache_mem_k[0].reshape(bs_, m, d), cache_mem_v[0].reshape(bs_, m, d), wts, ts)

    hd = d // MEM_HEADS
    conv_s = jnp.concatenate([state_conv[0][:, ts:], glu_s[:, :ts]], axis=1)
    return (y_p, y_s, glu_p[:, tp - hist_rows:][None], mk_p.reshape(bp, m, MEM_HEADS, hd)[None],
            mv_p.reshape(bp, m, MEM_HEADS, hd)[None], conv_s[None], vn_s[:, :ts][None])
```

```python
import functools

import jax
import jax.numpy as jnp
from jax import lax
from jax.experimental import pallas as pl
from jax.experimental.pallas import tpu as pltpu

EPS = 1e-6
CHUNK = 64
GMLP_CHUNK = 128
GMLP_GROUPS = 4
CONV_K = 31
MEM_HEADS = 4
PEER_HEADS = 8
PEER_NKEYS = 128
PEER_TOPK = 16
HALO = 32
GATE_ROWGROUPS = 4
EXPERT_PARTS = 2
CHUNK_KEYS = 16
VMEM_LIMIT_EXPERTS = 56 * 1024 * 1024

F32 = jnp.float32
BF16 = jnp.bfloat16
VMEM_LIMIT = 48 * 1024 * 1024


def _cparams(sem):
    return pltpu.CompilerParams(dimension_semantics=sem, vmem_limit_bytes=VMEM_LIMIT)


def _rms(x, g):
    r = lax.rsqrt(jnp.mean(x * x, axis=-1, keepdims=True) + EPS)
    return (x * r) * g


def _layernorm(x, g, b):
    mu = jnp.mean(x, axis=-1, keepdims=True)
    d = x - mu
    var = jnp.mean(d * d, axis=-1, keepdims=True)
    return (d * lax.rsqrt(var + EPS)) * g + b


def _gelu_tanh(x):
    c = 0.7978845608028654
    inner = x * (c + (c * 0.044715) * (x * x))
    return (0.5 * x) * (1.0 + jnp.tanh(inner))


def _dot(a, b):
    return jnp.dot(a, b, preferred_element_type=F32)


def _full(shape):
    n = len(shape)
    return pl.BlockSpec(shape, lambda *_: (0,) * n)


def _norm_matmul_kernel(x_ref, g_ref, w_ref, o_ref):
    h = _rms(x_ref[...], g_ref[...]).astype(BF16)
    o_ref[...] = _dot(h, w_ref[...])


def _norm_matmul(x, g, w, tm):
    n, d = x.shape
    m = w.shape[1]
    return pl.pallas_call(
        _norm_matmul_kernel,
        grid=(n // tm,),
        in_specs=[pl.BlockSpec((tm, d), lambda i: (i, 0)), _full((1, d)), _full((d, m))],
        out_specs=pl.BlockSpec((tm, m), lambda i: (i, 0)),
        out_shape=jax.ShapeDtypeStruct((n, m), F32),
        compiler_params=_cparams(("parallel",)),
        name="mem_kv",
    )(x, g, w)


def _proj_a_kernel(x_ref, g_ref, w_ref, lg_ref, lb_ref, u_ref, vn_ref):
    h = _rms(x_ref[...], g_ref[...]).astype(BF16)
    d = u_ref.shape[1]
    u_ref[...] = _gelu_tanh(_dot(h, w_ref[:, :d])).astype(u_ref.dtype)
    v = _gelu_tanh(_dot(h, w_ref[:, d:]))
    vn_ref[...] = _layernorm(v, lg_ref[...], lb_ref[...])


def _proj_b_kernel(x_ref, g_ref, w_ref, glu_ref):
    h = _rms(x_ref[...], g_ref[...]).astype(BF16)
    d = glu_ref.shape[1]
    glu_ref[...] = _dot(h, w_ref[:, :d]) * jax.nn.sigmoid(_dot(h, w_ref[:, d:]))


def _proj_cg_kernel(x_ref, g_ref, w_ref, bg_ref, q_ref, gates_ref):
    h = _rms(x_ref[...], g_ref[...]).astype(BF16)
    d = q_ref.shape[1]
    q_ref[...] = _dot(h, w_ref[:, :d]).astype(q_ref.dtype)
    for k in range(gates_ref.shape[1] // d):
        z = _dot(h, w_ref[:, (k + 1) * d:(k + 2) * d]) + bg_ref[:, k * d:(k + 1) * d]
        gates_ref[:, k * d:(k + 1) * d] = jax.nn.sigmoid(z).astype(gates_ref.dtype)


def _in_proj(x, g_mix, w_a, w_b, w_cg, b_gate, ln_g, ln_b, tm):
    n, d = x.shape
    row = lambda w: pl.BlockSpec((tm, w), lambda i: (i, 0))
    grid = (n // tm,)
    u, vn = pl.pallas_call(
        _proj_a_kernel, grid=grid,
        in_specs=[row(d), _full((1, d)), _full(w_a.shape), _full((1, d)), _full((1, d))],
        out_specs=[row(d), row(d)],
        out_shape=[jax.ShapeDtypeStruct((n, d), BF16), jax.ShapeDtypeStruct((n, d), F32)],
        compiler_params=_cparams(("parallel",)), name="in_proj_gmlp",
    )(x, g_mix, w_a, ln_g, ln_b)
    glu = pl.pallas_call(
        _proj_b_kernel, grid=grid,
        in_specs=[row(d), _full((1, d)), _full(w_b.shape)],
        out_specs=row(d),
        out_shape=jax.ShapeDtypeStruct((n, d), F32),
        compiler_params=_cparams(("parallel",)), name="in_proj_conv",
    )(x, g_mix, w_b)
    ng = b_gate.shape[1]
    q, gates = pl.pallas_call(
        _proj_cg_kernel, grid=grid,
        in_specs=[row(d), _full((1, d)), _full(w_cg.shape), _full((1, ng))],
        out_specs=[row(d), row(ng)],
        out_shape=[jax.ShapeDtypeStruct((n, d), BF16), jax.ShapeDtypeStruct((n, ng), BF16)],
        compiler_params=_cparams(("parallel",)), name="in_proj_attn_gates",
    )(x, g_mix, w_cg, b_gate)
    return u, vn, glu, q, gates


def _mixer_kernel(x_ref, u_ref, vn_ref, glu_ref, halo_ref, hist_ref, q_ref, gates_ref, mk_ref, mv_ref,
                  ws_ref, bs_ref, wa_ref, cw_ref, cb_ref, clg_ref, clb_ref, wb_ref, wc_ref, wo_ref,
                  o_ref, ext_ref, mix_ref, *, n_valid):
    tq, d = x_ref.shape
    i = pl.program_id(1)

    @pl.when(i == 0)
    def _():
        ext_ref[0:HALO, :] = hist_ref[...]

    @pl.when(i > 0)
    def _():
        ext_ref[0:HALO, :] = halo_ref[...]

    ext_ref[HALO:HALO + tq, :] = glu_ref[...]
    off = HALO - (CONV_K - 1)
    c = cb_ref[...] + cw_ref[0:1, :] * ext_ref[off:off + tq, :]
    for k in range(1, CONV_K):
        c = c + cw_ref[k:k + 1, :] * ext_ref[off + k:off + k + tq, :]
    yb_in = jax.nn.silu(_layernorm(c, clg_ref[...], clb_ref[...])).astype(BF16)
    y_b = _dot(yb_in, wb_ref[...])

    ii = lax.broadcasted_iota(jnp.int32, (GMLP_CHUNK, GMLP_CHUNK), 0)
    jj = lax.broadcasted_iota(jnp.int32, (GMLP_CHUNK, GMLP_CHUNK), 1)
    shift = CHUNK.bit_length() - 1
    mask = ((jj >> shift) <= (ii >> shift)) & (ii < n_valid) & (jj < n_valid)
    gd = d // GMLP_GROUPS
    for g in range(GMLP_GROUPS):
        wg = jnp.where(mask, ws_ref[g], 0.0).astype(BF16)
        for cgrp in range(tq // GMLP_CHUNK):
            rs = slice(cgrp * GMLP_CHUNK, (cgrp + 1) * GMLP_CHUNK)
            cs = slice(g * gd, (g + 1) * gd)
            sp = _dot(wg, vn_ref[rs, cs].astype(BF16)) + bs_ref[g]
            mix_ref[rs, cs] = (u_ref[rs, cs].astype(F32) * sp).astype(BF16)
    y_a = _dot(mix_ref[...], wa_ref[...])

    hd = d // MEM_HEADS
    scale = hd ** -0.5
    for h in range(MEM_HEADS):
        cs = slice(h * hd, (h + 1) * hd)
        kh = mk_ref[:, cs].astype(BF16)
        vh = mv_ref[:, cs].astype(BF16)
        s = lax.dot_general(q_ref[:, cs], kh, (((1,), (1,)), ((), ())), preferred_element_type=F32) * scale
        p = jnp.exp(s - jnp.max(s, axis=-1, keepdims=True))
        p = p / jnp.sum(p, axis=-1, keepdims=True)
        mix_ref[:, cs] = _dot(p.astype(BF16), vh).astype(BF16)
    y_c = _dot(mix_ref[...], wc_ref[...])

    merged = (gates_ref[:, 0:d].astype(F32) * y_a + gates_ref[:, d:2 * d].astype(F32) * y_b
              + gates_ref[:, 2 * d:3 * d].astype(F32) * y_c)
    o_ref[...] = x_ref[...] + _dot(merged.astype(BF16), wo_ref[...])


def _mixer(x, u, vn, glu, hist, q, gates, mem_k, mem_v, ws, bs, w_a_out, conv_w, conv_b, conv_ln_g, conv_ln_b,
           w_b_out, w_c_out, w_o, tq, n_valid):
    b, t, d = x.shape
    m = mem_k.shape[1]
    tile = lambda w: pl.BlockSpec((None, tq, w), lambda bi, i: (bi, i, 0))
    per_b = lambda r, w: pl.BlockSpec((None, r, w), lambda bi, i: (bi, 0, 0))
    halo = pl.BlockSpec((None, HALO, d), lambda bi, i: (bi, jnp.maximum(i * (tq // HALO) - 1, 0), 0))
    return pl.pallas_call(
        functools.partial(_mixer_kernel, n_valid=n_valid),
        grid=(b, t // tq),
        in_specs=[tile(d), tile(d), tile(d), tile(d), halo, per_b(HALO, d), tile(d), tile(3 * d),
                  per_b(m, d), per_b(m, d),
                  _full(ws.shape), _full(bs.shape), _full(w_a_out.shape), _full(conv_w.shape), _full((1, d)),
                  _full((1, d)), _full((1, d)), _full(w_b_out.shape), _full(w_c_out.shape), _full(w_o.shape)],
        out_specs=tile(d),
        out_shape=jax.ShapeDtypeStruct((b, t, d), F32),
        scratch_shapes=[pltpu.VMEM((HALO + tq, d), F32), pltpu.VMEM((tq, d), BF16)],
        compiler_params=_cparams(("parallel", "arbitrary")), name="mixer",
    )(x, u, vn, glu, glu, hist, q, gates, mem_k, mem_v, ws, bs, w_a_out, conv_w, conv_b, conv_ln_g, conv_ln_b,
      w_b_out, w_c_out, w_o)


def _sorting_network(n):
    out = []
    p = 1
    while p < n:
        k = p
        while k >= 1:
            for j in range(k % p, n - k, 2 * k):
                for i in range(min(k, n - j - k)):
                    if (i + j) // (2 * p) == (i + j + k) // (2 * p):
                        out.append((i + j, i + j + k))
            k //= 2
        p *= 2
    return out


def _top_values(s, out_ref):
    tiles = [s[r:r + 8, :] for r in range(0, s.shape[0], 8)]
    for i, j in _sorting_network(len(tiles)):
        tiles[i], tiles[j] = jnp.maximum(tiles[i], tiles[j]), jnp.minimum(tiles[i], tiles[j])
    depth = min(len(tiles), PEER_TOPK)
    tiles = tiles[:depth]
    for k in range(PEER_TOPK):
        m = jnp.max(tiles[0], axis=0, keepdims=True)
        out_ref[k:k + 1, :] = m
        hit = tiles[0] == m
        left = min(depth, PEER_TOPK - 1 - k)
        tiles = [jnp.where(hit, tiles[t + 1] if t + 1 < len(tiles) else -jnp.inf, tiles[t]) for t in range(left)]


def _candidates(va_ref, vb_ref, op, fill):
    half = PEER_TOPK // 2
    rows = lax.broadcasted_iota(jnp.int32, (half, va_ref.shape[1]), 0)
    pieces = [op(va_ref[0:1, :], vb_ref[...])]
    for a in range(1, half):
        nb = PEER_TOPK // (a + 1)
        pieces.append(jnp.where(rows < nb, op(va_ref[a:a + 1, :], vb_ref[0:half, :]), fill))
    pieces.append(op(va_ref[half:PEER_TOPK, :], vb_ref[0:1, :]))
    return pieces


def _col_max(pieces):
    tiles = []
    for p in pieces:
        tiles.extend(p[r:r + 8, :] for r in range(0, p.shape[0], 8))
    return jnp.max(functools.reduce(jnp.maximum, tiles), axis=0, keepdims=True)


def _route_kernel(x_ref, g_ref, wqt_ref, k1_ref, k2_ref, hnt_ref, sd_ref, e1n_ref, et2_ref,
                  v1_ref, v2_ref, ev1_ref, ev2_ref):
    hn_t = _rms(x_ref[...], g_ref[...]).T.astype(BF16)
    hnt_ref[...] = hn_t
    q_t = _dot(wqt_ref[...], hn_t)
    nk = PEER_NKEYS
    neg = -jnp.inf
    for h in range(PEER_HEADS):
        s1 = _dot(k1_ref[...], q_t[2 * h * nk:(2 * h + 1) * nk, :].astype(BF16))
        s2 = _dot(k2_ref[...], q_t[(2 * h + 1) * nk:(2 * h + 2) * nk, :].astype(BF16))
        _top_values(s1, v1_ref)
        _top_values(s2, v2_ref)
        m1 = v1_ref[0:1, :]
        m2 = v2_ref[0:1, :]
        cur = _candidates(v1_ref, v2_ref, lambda a, b: a + b, neg)
        tau = None
        for k in range(PEER_TOPK):
            tau = _col_max(cur)
            if k + 1 < PEER_TOPK:
                cur = [jnp.where(p == tau, neg, p) for p in cur]
        ev1_ref[...] = jnp.exp(v1_ref[...] - m1)
        ev2_ref[...] = jnp.exp(v2_ref[...] - m2)
        sums = _candidates(v1_ref, v2_ref, lambda a, b: a + b, neg)
        prods = _candidates(ev1_ref, ev2_ref, lambda a, b: a * b, 0.0)
        z = functools.reduce(
            lambda a, b: a + b,
            [jnp.sum(jnp.where(sm >= tau, pr, 0.0), axis=0, keepdims=True) for sm, pr in zip(sums, prods)])
        half = PEER_TOPK // 2
        ev2_pieces = ([ev2_ref[...]] + [ev2_ref[0:half, :]] * (half - 1)
                      + [jnp.broadcast_to(ev2_ref[0:1, :], (half, s1.shape[1]))])
        thr = [jnp.where(sm >= tau, ev, jnp.inf) for sm, ev in zip(sums, ev2_pieces)]
        et2_rank = [jnp.min(t, axis=0, keepdims=True) for t in thr[:half]] + [thr[half][a:a + 1, :] for a in range(half)]
        et2 = jnp.full(s1.shape, jnp.inf, F32)
        for a in reversed(range(PEER_TOPK)):
            et2 = jnp.where(s1 >= v1_ref[a:a + 1, :], et2_rank[a], et2)
        sd = s2 - m2
        for j in range(sd_ref.shape[0]):
            sd_ref[j, h] = sd[:, j * 128:(j + 1) * 128]
        e1n_ref[h] = jnp.exp(s1 - m1) / z
        et2_ref[h] = et2


def _route(x1, g_ffn, wq_t, k1, k2, tt):
    n, d = x1.shape
    hk = PEER_HEADS
    nk = PEER_NKEYS
    sc = pl.BlockSpec((hk, nk, tt), lambda i: (0, 0, i))
    sc_shape = jax.ShapeDtypeStruct((hk, nk, n), F32)
    dense = pl.BlockSpec((tt // 128, hk, nk, 128), lambda i: (i, 0, 0, 0))
    dense_shape = jax.ShapeDtypeStruct((n // 128, hk, nk, 128), F32)
    return pl.pallas_call(
        _route_kernel, grid=(n // tt,),
        in_specs=[pl.BlockSpec((tt, d), lambda i: (i, 0)), _full((1, d)), _full(wq_t.shape), _full(k1.shape),
                  _full(k2.shape)],
        out_specs=[pl.BlockSpec((d, tt), lambda i: (0, i)), dense, sc, sc],
        out_shape=[jax.ShapeDtypeStruct((d, n), BF16), dense_shape, sc_shape, sc_shape],
        scratch_shapes=[pltpu.VMEM((PEER_TOPK, tt), F32)] * 4,
        compiler_params=_cparams(("parallel",)), name="peer_route",
    )(x1, g_ffn, wq_t, k1, k2)


def _experts_kernel(hnt_ref, s2_ref, e1n_ref, et2_ref, u_ref, vt_ref, x1_ref, go_ref, y_ref,
                    acc_ref, *part_refs):
    c = pl.program_id(1)
    nparts = len(part_refs) // 2
    a_refs, w_refs = part_refs[:nparts], part_refs[nparts:]
    hc, tb = a_refs[0].shape
    nk = PEER_NKEYS
    slabs = hc // nk
    rg = GATE_ROWGROUPS

    @pl.when(c == 0)
    def _():
        acc_ref[...] = jnp.zeros_like(acc_ref)

    def gated_activations(half, a_ref, w_ref):
        for l in range(tb // 128):
            ls = slice(l * 128, (l + 1) * 128)
            e1g = [e1n_ref[h, :, ls] for h in range(PEER_HEADS)]
            t2g = [et2_ref[h, :, ls] for h in range(PEER_HEADS)]
            for rt in range(nk // (8 * rg)):
                rows = [slice((rt * rg + r) * 8, (rt * rg + r + 1) * 8) for r in range(rg)]
                gate = [[None] * rg for _ in range(slabs)]
                for h in range(PEER_HEADS):
                    e2v = [jnp.exp(s2_ref[l, h, rows[r], :]) for r in range(rg)]
                    for s in range(slabs):
                        k = half * slabs + s
                        t2b = jnp.broadcast_to(t2g[h][k:k + 1, :], (8, 128))
                        e1b = jnp.broadcast_to(e1g[h][k:k + 1, :], (8, 128))
                        for r in range(rg):
                            term = e1b * jnp.where(e2v[r] >= t2b, e2v[r], 0.0)
                            gate[s][r] = term if gate[s][r] is None else gate[s][r] + term
                for s in range(slabs):
                    for r in range(0, rg, 2):
                        er = slice(s * nk + rows[r].start, s * nk + rows[r + 1].stop)
                        g2 = jnp.concatenate([gate[s][r], gate[s][r + 1]], axis=0)
                        w_ref[er, ls] = (_gelu_tanh(a_ref[er, ls]) * g2).astype(BF16)

    for p in range(nparts):
        a_refs[p][...] = _dot(u_ref[p * hc:(p + 1) * hc, :], hnt_ref[...])
    for p in range(nparts):
        gated_activations(p, a_refs[p], w_refs[p])
        acc_ref[...] += _dot(vt_ref[:, p * hc:(p + 1) * hc], w_refs[p][...])

    @pl.when(c == pl.num_programs(1) - 1)
    def _():
        x2 = x1_ref[...] + acc_ref[...].T
        y_ref[...] = _rms(x2, go_ref[...])


def _experts(hn_t, s2, e1n, et2, u_bf, vt_bf, x1, g_out, tb):
    n, d = x1.shape
    ne = u_bf.shape[0]
    nk = PEER_NKEYS
    ec = CHUNK_KEYS * nk
    hc = ec // EXPERT_PARTS
    dense = pl.BlockSpec((tb // 128, PEER_HEADS, nk, 128), lambda i, c: (i, 0, 0, 0))
    keyrow = pl.BlockSpec((PEER_HEADS, CHUNK_KEYS, tb), lambda i, c: (0, c, i))
    return pl.pallas_call(
        _experts_kernel, grid=(n // tb, ne // ec),
        in_specs=[pl.BlockSpec((d, tb), lambda i, c: (0, i)), dense, keyrow, keyrow,
                  pl.BlockSpec((ec, d), lambda i, c: (c, 0)), pl.BlockSpec((None, d, ec), lambda i, c: (c, 0, 0)),
                  pl.BlockSpec((tb, d), lambda i, c: (i, 0)), _full((1, d))],
        out_specs=pl.BlockSpec((tb, d), lambda i, c: (i, 0)),
        out_shape=jax.ShapeDtypeStruct((n, d), F32),
        scratch_shapes=([pltpu.VMEM((d, tb), F32)] + [pltpu.VMEM((hc, tb), F32)] * EXPERT_PARTS
                        + [pltpu.VMEM((hc, tb), BF16)] * EXPERT_PARTS),
        compiler_params=pltpu.CompilerParams(dimension_semantics=("parallel", "arbitrary"),
                                             vmem_limit_bytes=VMEM_LIMIT_EXPERTS),
        name="peer_experts",
    )(hn_t, s2, e1n, et2, u_bf, vt_bf, x1, g_out)


def _tile_for(n, pref):
    t = pref
    while n % t:
        t //= 2
    return t


def _layer(x, hist, mem_k, mem_v, wts, t_valid):
    b, t, d = x.shape
    n = b * t
    xf = x.reshape(n, d)
    u, vn, glu, q, gates = _in_proj(xf, wts["g_mix"], wts["w_a"], wts["w_b"], wts["w_cg"], wts["b_gate"],
                                    wts["ln_g"], wts["ln_b"], _tile_for(n, 512))
    r3 = lambda a: a.reshape(b, t, a.shape[-1])
    x1 = _mixer(x, r3(u), r3(vn), r3(glu), hist, r3(q), r3(gates), mem_k, mem_v, wts["ws"], wts["bs"],
                wts["w_a_out"], wts["conv_w"], wts["conv_b"], wts["conv_ln_g"], wts["conv_ln_b"], wts["w_b_out"],
                wts["w_c_out"], wts["w_o"], _tile_for(t, 256), min(t_valid, GMLP_CHUNK))
    x1f = x1[:, :t_valid].reshape(b * t_valid, d)
    nv = b * t_valid
    hn_t, sd, e1n, et2 = _route(x1f, wts["g_ffn"], wts["wq_t"], wts["k1"], wts["k2"], _tile_for(nv, 256))
    y = _experts(hn_t, sd, e1n, et2, wts["u"], wts["v_t"], x1f, wts["g_out"], _tile_for(nv, 512))
    return y.reshape(b, t_valid, d), r3(glu), r3(vn)


def kernel(x_prompt, mem_prompt, x_sample, cache_mem_k, cache_mem_v, state_conv, norm_mix_g, w_in, b_gate, gmlp_ln_g, gmlp_ln_b, gmlp_ws, gmlp_bs, w_a_out, conv_w, conv_b, conv_ln_g, conv_ln_b, w_b_out, norm_mem_g, w_mem_kv, w_c_out, w_o, norm_ffn_g, peer_wq, peer_k1, peer_k2, peer_u, peer_v, norm_out_g):
    assert w_in.shape[0] == 1, "single trunk layer"
    bp, tp, d = x_prompt.shape
    bs_, ts, _ = x_sample.shape
    m = mem_prompt.shape[1]
    hist_rows = CONV_K - 1
    assert tp % GMLP_CHUNK == 0 and ts <= CHUNK and ts <= hist_rows
    row = lambda a: a.reshape(1, -1)
    w_in_b = w_in[0].astype(BF16)
    wts = dict(
        g_mix=row(norm_mix_g[0]), w_a=w_in_b[:, :2 * d], w_b=w_in_b[:, 2 * d:4 * d], w_cg=w_in_b[:, 4 * d:],
        b_gate=row(b_gate[0]), ln_g=row(gmlp_ln_g[0]), ln_b=row(gmlp_ln_b[0]),
        ws=gmlp_ws[0], bs=gmlp_bs[0][:, :, None], w_a_out=w_a_out[0].astype(BF16), conv_w=conv_w[0],
        conv_b=row(conv_b[0]), conv_ln_g=row(conv_ln_g[0]), conv_ln_b=row(conv_ln_b[0]),
        w_b_out=w_b_out[0].astype(BF16), w_c_out=w_c_out[0].astype(BF16), w_o=w_o[0].astype(BF16),
        g_ffn=row(norm_ffn_g[0]), wq_t=peer_wq[0].T.astype(BF16), k1=peer_k1[0].astype(BF16),
        k2=peer_k2[0].astype(BF16), u=peer_u[0].astype(BF16), g_out=row(norm_out_g),
        v_t=jnp.swapaxes(peer_v[0].reshape(-1, CHUNK_KEYS * PEER_NKEYS, d), 1, 2).astype(BF16),
    )

    kv = _norm_matmul(mem_prompt.reshape(bp * m, d), row(norm_mem_g[0]), w_mem_kv[0].astype(BF16), _tile_for(bp * m, 512))
    kv = kv.reshape(bp, m, 2 * d)
    mk_p, mv_p = kv[:, :, :d], kv[:, :, d:]
    y_p, glu_p, _ = _layer(x_prompt, jnp.zeros((bp, HALO, d), F32), mk_p, mv_p, wts, tp)

    xs = jnp.pad(x_sample, ((0, 0), (0, GMLP_CHUNK - ts), (0, 0)))
    hist_s = jnp.pad(state_conv[0], ((0, 0), (HALO - hist_rows, 0), (0, 0)))
    y_s, glu_s, vn_s = _layer(xs, hist_s, cache_mem_k[0].reshape(bs_, m, d), cache_mem_v[0].reshape(bs_, m, d), wts, ts)

    hd = d // MEM_HEADS
    conv_s = jnp.concatenate([state_conv[0][:, ts:], glu_s[:, :ts]], axis=1)
    return (y_p, y_s, glu_p[:, tp - hist_rows:][None], mk_p.reshape(bp, m, MEM_HEADS, hd)[None],
            mv_p.reshape(bp, m, MEM_HEADS, hd)[None], conv_s[None], vn_s[:, :ts][None])
```

```python
import functools

import jax
import jax.numpy as jnp
from jax import lax
from jax.experimental import pallas as pl
from jax.experimental.pallas import tpu as pltpu

EPS = 1e-6
CHUNK = 64
GMLP_CHUNK = 128
GMLP_GROUPS = 4
CONV_K = 31
MEM_HEADS = 4
PEER_HEADS = 8
PEER_NKEYS = 128
PEER_TOPK = 16
HALO = 32
GATE_ROWGROUPS = 4
EXPERT_PARTS = 2
CHUNK_KEYS = 16
VMEM_LIMIT_EXPERTS = 56 * 1024 * 1024

F32 = jnp.float32
BF16 = jnp.bfloat16
VMEM_LIMIT = 48 * 1024 * 1024


def _cparams(sem):
    return pltpu.CompilerParams(dimension_semantics=sem, vmem_limit_bytes=VMEM_LIMIT)


def _rms(x, g):
    r = lax.rsqrt(jnp.mean(x * x, axis=-1, keepdims=True) + EPS)
    return (x * r) * g


def _layernorm(x, g, b):
    mu = jnp.mean(x, axis=-1, keepdims=True)
    d = x - mu
    var = jnp.mean(d * d, axis=-1, keepdims=True)
    return (d * lax.rsqrt(var + EPS)) * g + b


def _gelu_tanh(x):
    c = 0.7978845608028654
    inner = x * (c + (c * 0.044715) * (x * x))
    return (0.5 * x) * (1.0 + jnp.tanh(inner))


def _dot(a, b):
    return jnp.dot(a, b, preferred_element_type=F32)


def _full(shape):
    n = len(shape)
    return pl.BlockSpec(shape, lambda *_: (0,) * n)


def _norm_matmul_kernel(x_ref, g_ref, w_ref, o_ref):
    h = _rms(x_ref[...], g_ref[...]).astype(BF16)
    o_ref[...] = _dot(h, w_ref[...])


def _norm_matmul(x, g, w, tm):
    n, d = x.shape
    m = w.shape[1]
    return pl.pallas_call(
        _norm_matmul_kernel,
        grid=(n // tm,),
        in_specs=[pl.BlockSpec((tm, d), lambda i: (i, 0)), _full((1, d)), _full((d, m))],
        out_specs=pl.BlockSpec((tm, m), lambda i: (i, 0)),
        out_shape=jax.ShapeDtypeStruct((n, m), F32),
        compiler_params=_cparams(("parallel",)),
        name="mem_kv",
    )(x, g, w)


def _proj_a_kernel(x_ref, g_ref, w_ref, lg_ref, lb_ref, u_ref, vn_ref):
    h = _rms(x_ref[...], g_ref[...]).astype(BF16)
    d = u_ref.shape[1]
    u_ref[...] = _gelu_tanh(_dot(h, w_ref[:, :d])).astype(u_ref.dtype)
    v = _gelu_tanh(_dot(h, w_ref[:, d:]))
    vn_ref[...] = _layernorm(v, lg_ref[...], lb_ref[...])


def _proj_b_kernel(x_ref, g_ref, w_ref, glu_ref):
    h = _rms(x_ref[...], g_ref[...]).astype(BF16)
    d = glu_ref.shape[1]
    glu_ref[...] = _dot(h, w_ref[:, :d]) * jax.nn.sigmoid(_dot(h, w_ref[:, d:]))


def _proj_cg_kernel(x_ref, g_ref, w_ref, bg_ref, q_ref, gates_ref):
    h = _rms(x_ref[...], g_ref[...]).astype(BF16)
    d = q_ref.shape[1]
    q_ref[...] = _dot(h, w_ref[:, :d]).astype(q_ref.dtype)
    for k in range(gates_ref.shape[1] // d):
        z = _dot(h, w_ref[:, (k + 1) * d:(k + 2) * d]) + bg_ref[:, k * d:(k + 1) * d]
        gates_ref[:, k * d:(k + 1) * d] = jax.nn.sigmoid(z).astype(gates_ref.dtype)


def _in_proj(x, g_mix, w_a, w_b, w_cg, b_gate, ln_g, ln_b, tm):
    n, d = x.shape
    row = lambda w: pl.BlockSpec((tm, w), lambda i: (i, 0))
    grid = (n // tm,)
    u, vn = pl.pallas_call(
        _proj_a_kernel, grid=grid,
        in_specs=[row(d), _full((1, d)), _full(w_a.shape), _full((1, d)), _full((1, d))],
        out_specs=[row(d), row(d)],
        out_shape=[jax.ShapeDtypeStruct((n, d), BF16), jax.ShapeDtypeStruct((n, d), F32)],
        compiler_params=_cparams(("parallel",)), name="in_proj_gmlp",
    )(x, g_mix, w_a, ln_g, ln_b)
    glu = pl.pallas_call(
        _proj_b_kernel, grid=grid,
        in_specs=[row(d), _full((1, d)), _full(w_b.shape)],
        out_specs=row(d),
        out_shape=jax.ShapeDtypeStruct((n, d), F32),
        compiler_params=_cparams(("parallel",)), name="in_proj_conv",
    )(x, g_mix, w_b)
    ng = b_gate.shape[1]
    q, gates = pl.pallas_call(
        _proj_cg_kernel, grid=grid,
        in_specs=[row(d), _full((1, d)), _full(w_cg.shape), _full((1, ng))],
        out_specs=[row(d), row(ng)],
        out_shape=[jax.ShapeDtypeStruct((n, d), BF16), jax.ShapeDtypeStruct((n, ng), BF16)],
        compiler_params=_cparams(("parallel",)), name="in_proj_attn_gates",
    )(x, g_mix, w_cg, b_gate)
    return u, vn, glu, q, gates


def _mixer_kernel(x_ref, u_ref, vn_ref, glu_ref, halo_ref, hist_ref, q_ref, gates_ref, mk_ref, mv_ref,
                  ws_ref, bs_ref, wa_ref, cw_ref, cb_ref, clg_ref, clb_ref, wb_ref, wc_ref, wo_ref,
                  o_ref, ext_ref, mix_ref, *, n_valid):
    tq, d = x_ref.shape
    i = pl.program_id(1)

    @pl.when(i == 0)
    def _():
        ext_ref[0:HALO, :] = hist_ref[...]

    @pl.when(i > 0)
    def _():
        ext_ref[0:HALO, :] = halo_ref[...]

    ext_ref[HALO:HALO + tq, :] = glu_ref[...]
    off = HALO - (CONV_K - 1)
    c = cb_ref[...] + cw_ref[0:1, :] * ext_ref[off:off + tq, :]
    for k in range(1, CONV_K):
        c = c + cw_ref[k:k + 1, :] * ext_ref[off + k:off + k + tq, :]
    yb_in = jax.nn.silu(_layernorm(c, clg_ref[...], clb_ref[...])).astype(BF16)
    y_b = _dot(yb_in, wb_ref[...])

    ii = lax.broadcasted_iota(jnp.int32, (GMLP_CHUNK, GMLP_CHUNK), 0)
    jj = lax.broadcasted_iota(jnp.int32, (GMLP_CHUNK, GMLP_CHUNK), 1)
    shift = CHUNK.bit_length() - 1
    mask = ((jj >> shift) <= (ii >> shift)) & (ii < n_valid) & (jj < n_valid)
    gd = d // GMLP_GROUPS
    for g in range(GMLP_GROUPS):
        wg = jnp.where(mask, ws_ref[g], 0.0).astype(BF16)
        for cgrp in range(tq // GMLP_CHUNK):
            rs = slice(cgrp * GMLP_CHUNK, (cgrp + 1) * GMLP_CHUNK)
            cs = slice(g * gd, (g + 1) * gd)
            sp = _dot(wg, vn_ref[rs, cs].astype(BF16)) + bs_ref[g]
            mix_ref[rs, cs] = (u_ref[rs, cs].astype(F32) * sp).astype(BF16)
    y_a = _dot(mix_ref[...], wa_ref[...])

    hd = d // MEM_HEADS
    scale = hd ** -0.5
    for h in range(MEM_HEADS):
        cs = slice(h * hd, (h + 1) * hd)
        kh = mk_ref[:, cs].astype(BF16)
        vh = mv_ref[:, cs].astype(BF16)
        s = lax.dot_general(q_ref[:, cs], kh, (((1,), (1,)), ((), ())), preferred_element_type=F32) * scale
        p = jnp.exp(s - jnp.max(s, axis=-1, keepdims=True))
        p = p / jnp.sum(p, axis=-1, keepdims=True)
        mix_ref[:, cs] = _dot(p.astype(BF16), vh).astype(BF16)
    y_c = _dot(mix_ref[...], wc_ref[...])

    merged = (gates_ref[:, 0:d].astype(F32) * y_a + gates_ref[:, d:2 * d].astype(F32) * y_b
              + gates_ref[:, 2 * d:3 * d].astype(F32) * y_c)
    o_ref[...] = x_ref[...] + _dot(merged.astype(BF16), wo_ref[...])


def _mixer(x, u, vn, glu, hist, q, gates, mem_k, mem_v, ws, bs, w_a_out, conv_w, conv_b, conv_ln_g, conv_ln_b,
           w_b_out, w_c_out, w_o, tq, n_valid):
    b, t, d = x.shape
    m = mem_k.shape[1]
    tile = lambda w: pl.BlockSpec((None, tq, w), lambda bi, i: (bi, i, 0))
    per_b = lambda r, w: pl.BlockSpec((None, r, w), lambda bi, i: (bi, 0, 0))
    halo = pl.BlockSpec((None, HALO, d), lambda bi, i: (bi, jnp.maximum(i * (tq // HALO) - 1, 0), 0))
    return pl.pallas_call(
        functools.partial(_mixer_kernel, n_valid=n_valid),
        grid=(b, t // tq),
        in_specs=[tile(d), tile(d), tile(d), tile(d), halo, per_b(HALO, d), tile(d), tile(3 * d),
                  per_b(m, d), per_b(m, d),
                  _full(ws.shape), _full(bs.shape), _full(w_a_out.shape), _full(conv_w.shape), _full((1, d)),
                  _full((1, d)), _full((1, d)), _full(w_b_out.shape), _full(w_c_out.shape), _full(w_o.shape)],
        out_specs=tile(d),
        out_shape=jax.ShapeDtypeStruct((b, t, d), F32),
        scratch_shapes=[pltpu.VMEM((HALO + tq, d), F32), pltpu.VMEM((tq, d), BF16)],
        compiler_params=_cparams(("parallel", "arbitrary")), name="mixer",
    )(x, u, vn, glu, glu, hist, q, gates, mem_k, mem_v, ws, bs, w_a_out, conv_w, conv_b, conv_ln_g, conv_ln_b,
      w_b_out, w_c_out, w_o)


def _sorting_network(n):
    out = []
    p = 1
    while p < n:
        k = p
        while k >= 1:
            for j in range(k % p, n - k, 2 * k):
                for i in range(min(k, n - j - k)):
                    if (i + j) // (2 * p) == (i + j + k) // (2 * p):
                        out.append((i + j, i + j + k))
            k //= 2
        p *= 2
    return out


def _top_values(s, out_ref):
    tiles = [s[r:r + 8, :] for r in range(0, s.shape[0], 8)]
    for k, m in enumerate(_pop_largest(tiles, PEER_TOPK)):
        out_ref[k:k + 1, :] = m


def _pop_largest(tiles, count):
    tiles = list(tiles)
    padded = 1 << (len(tiles) - 1).bit_length()
    for i, j in _sorting_network(padded):
        if j < len(tiles):
            tiles[i], tiles[j] = jnp.maximum(tiles[i], tiles[j]), jnp.minimum(tiles[i], tiles[j])
    tiles = tiles[:count]
    out = []
    for k in range(count):
        m = jnp.max(tiles[0], axis=0, keepdims=True)
        out.append(m)
        hit = tiles[0] == m
        left = min(len(tiles), count - 1 - k)
        tiles = [jnp.where(hit, tiles[t + 1] if t + 1 < len(tiles) else -jnp.inf, tiles[t]) for t in range(left)]
    return out


def _candidates(va_ref, vb_ref, op, fill):
    half = PEER_TOPK // 2
    rows = lax.broadcasted_iota(jnp.int32, (half, va_ref.shape[1]), 0)
    pieces = [op(va_ref[0:1, :], vb_ref[...])]
    for a in range(1, half):
        nb = PEER_TOPK // (a + 1)
        pieces.append(jnp.where(rows < nb, op(va_ref[a:a + 1, :], vb_ref[0:half, :]), fill))
    pieces.append(op(va_ref[half:PEER_TOPK, :], vb_ref[0:1, :]))
    return pieces


def _route_kernel(x_ref, g_ref, wqt_ref, k1_ref, k2_ref, hnt_ref, sd_ref, e1n_ref, et2_ref,
                  v1_ref, v2_ref, ev1_ref, ev2_ref):
    hn_t = _rms(x_ref[...], g_ref[...]).T.astype(BF16)
    hnt_ref[...] = hn_t
    q_t = _dot(wqt_ref[...], hn_t)
    nk = PEER_NKEYS
    neg = -jnp.inf
    for h in range(PEER_HEADS):
        s1 = _dot(k1_ref[...], q_t[2 * h * nk:(2 * h + 1) * nk, :].astype(BF16))
        s2 = _dot(k2_ref[...], q_t[(2 * h + 1) * nk:(2 * h + 2) * nk, :].astype(BF16))
        _top_values(s1, v1_ref)
        _top_values(s2, v2_ref)
        m1 = v1_ref[0:1, :]
        m2 = v2_ref[0:1, :]
        cand_tiles = []
        for p in _candidates(v1_ref, v2_ref, lambda a, b: a + b, neg):
            cand_tiles.extend(p[r:r + 8, :] for r in range(0, p.shape[0], 8))
        tau = _pop_largest(cand_tiles, PEER_TOPK)[-1]
        ev1_ref[...] = jnp.exp(v1_ref[...] - m1)
        ev2_ref[...] = jnp.exp(v2_ref[...] - m2)
        sums = _candidates(v1_ref, v2_ref, lambda a, b: a + b, neg)
        prods = _candidates(ev1_ref, ev2_ref, lambda a, b: a * b, 0.0)
        z = functools.reduce(
            lambda a, b: a + b,
            [jnp.sum(jnp.where(sm >= tau, pr, 0.0), axis=0, keepdims=True) for sm, pr in zip(sums, prods)])
        half = PEER_TOPK // 2
        ev2_pieces = ([ev2_ref[...]] + [ev2_ref[0:half, :]] * (half - 1)
                      + [jnp.broadcast_to(ev2_ref[0:1, :], (half, s1.shape[1]))])
        thr = [jnp.where(sm >= tau, ev, jnp.inf) for sm, ev in zip(sums, ev2_pieces)]
        et2_rank = [jnp.min(t, axis=0, keepdims=True) for t in thr[:half]] + [thr[half][a:a + 1, :] for a in range(half)]
        et2 = jnp.full(s1.shape, jnp.inf, F32)
        for a in reversed(range(PEER_TOPK)):
            et2 = jnp.where(s1 >= v1_ref[a:a + 1, :], et2_rank[a], et2)
        sd = s2 - m2
        for j in range(sd_ref.shape[0]):
            sd_ref[j, h] = sd[:, j * 128:(j + 1) * 128]
        e1n_ref[h] = jnp.exp(s1 - m1) / z
        et2_ref[h] = et2


def _route(x1, g_ffn, wq_t, k1, k2, tt):
    n, d = x1.shape
    hk = PEER_HEADS
    nk = PEER_NKEYS
    sc = pl.BlockSpec((hk, nk, tt), lambda i: (0, 0, i))
    sc_shape = jax.ShapeDtypeStruct((hk, nk, n), F32)
    dense = pl.BlockSpec((tt // 128, hk, nk, 128), lambda i: (i, 0, 0, 0))
    dense_shape = jax.ShapeDtypeStruct((n // 128, hk, nk, 128), F32)
    return pl.pallas_call(
        _route_kernel, grid=(n // tt,),
        in_specs=[pl.BlockSpec((tt, d), lambda i: (i, 0)), _full((1, d)), _full(wq_t.shape), _full(k1.shape),
                  _full(k2.shape)],
        out_specs=[pl.BlockSpec((d, tt), lambda i: (0, i)), dense, sc, sc],
        out_shape=[jax.ShapeDtypeStruct((d, n), BF16), dense_shape, sc_shape, sc_shape],
        scratch_shapes=[pltpu.VMEM((PEER_TOPK, tt), F32)] * 4,
        compiler_params=_cparams(("parallel",)), name="peer_route",
    )(x1, g_ffn, wq_t, k1, k2)


def _experts_kernel(hnt_ref, s2_ref, e1n_ref, et2_ref, u_ref, vt_ref, x1_ref, go_ref, y_ref,
                    acc_ref, *part_refs):
    c = pl.program_id(1)
    nparts = len(part_refs) // 2
    a_refs, w_refs = part_refs[:nparts], part_refs[nparts:]
    hc, tb = a_refs[0].shape
    nk = PEER_NKEYS
    slabs = hc // nk
    rg = GATE_ROWGROUPS

    @pl.when(c == 0)
    def _():
        acc_ref[...] = jnp.zeros_like(acc_ref)

    def gated_activations(half, a_ref, w_ref):
        for l in range(tb // 128):
            ls = slice(l * 128, (l + 1) * 128)
            e1g = [e1n_ref[h, :, ls] for h in range(PEER_HEADS)]
            t2g = [et2_ref[h, :, ls] for h in range(PEER_HEADS)]
            for rt in range(nk // (8 * rg)):
                rows = [slice((rt * rg + r) * 8, (rt * rg + r + 1) * 8) for r in range(rg)]
                gate = [[None] * rg for _ in range(slabs)]
                for h in range(PEER_HEADS):
                    e2v = [jnp.exp(s2_ref[l, h, rows[r], :]) for r in range(rg)]
                    for s in range(slabs):
                        k = half * slabs + s
                        t2b = jnp.broadcast_to(t2g[h][k:k + 1, :], (8, 128))
                        e1b = jnp.broadcast_to(e1g[h][k:k + 1, :], (8, 128))
                        for r in range(rg):
                            term = e1b * jnp.where(e2v[r] >= t2b, e2v[r], 0.0)
                            gate[s][r] = term if gate[s][r] is None else gate[s][r] + term
                for s in range(slabs):
                    for r in range(0, rg, 2):
                        er = slice(s * nk + rows[r].start, s * nk + rows[r + 1].stop)
                        g2 = jnp.concatenate([gate[s][r], gate[s][r + 1]], axis=0)
                        w_ref[er, ls] = (_gelu_tanh(a_ref[er, ls]) * g2).astype(BF16)

    for p in range(nparts):
        a_refs[p][...] = _dot(u_ref[p * hc:(p + 1) * hc, :], hnt_ref[...])
    for p in range(nparts):
        gated_activations(p, a_refs[p], w_refs[p])
        acc_ref[...] += _dot(vt_ref[:, p * hc:(p + 1) * hc], w_refs[p][...])

    @pl.when(c == pl.num_programs(1) - 1)
    def _():
        x2 = x1_ref[...] + acc_ref[...].T
        y_ref[...] = _rms(x2, go_ref[...])


def _experts(hn_t, s2, e1n, et2, u_bf, vt_bf, x1, g_out, tb):
    n, d = x1.shape
    ne = u_bf.shape[0]
    nk = PEER_NKEYS
    ec = CHUNK_KEYS * nk
    hc = ec // EXPERT_PARTS
    dense = pl.BlockSpec((tb // 128, PEER_HEADS, nk, 128), lambda i, c: (i, 0, 0, 0))
    keyrow = pl.BlockSpec((PEER_HEADS, CHUNK_KEYS, tb), lambda i, c: (0, c, i))
    return pl.pallas_call(
        _experts_kernel, grid=(n // tb, ne // ec),
        in_specs=[pl.BlockSpec((d, tb), lambda i, c: (0, i)), dense, keyrow, keyrow,
                  pl.BlockSpec((ec, d), lambda i, c: (c, 0)), pl.BlockSpec((None, d, ec), lambda i, c: (c, 0, 0)),
                  pl.BlockSpec((tb, d), lambda i, c: (i, 0)), _full((1, d))],
        out_specs=pl.BlockSpec((tb, d), lambda i, c: (i, 0)),
        out_shape=jax.ShapeDtypeStruct((n, d), F32),
        scratch_shapes=([pltpu.VMEM((d, tb), F32)] + [pltpu.VMEM((hc, tb), F32)] * EXPERT_PARTS
                        + [pltpu.VMEM((hc, tb), BF16)] * EXPERT_PARTS),
        compiler_params=pltpu.CompilerParams(dimension_semantics=("parallel", "arbitrary"),
                                             vmem_limit_bytes=VMEM_LIMIT_EXPERTS),
        name="peer_experts",
    )(hn_t, s2, e1n, et2, u_bf, vt_bf, x1, g_out)


def _tile_for(n, pref):
    t = pref
    while n % t:
        t //= 2
    return t


def _layer(x, hist, mem_k, mem_v, wts, t_valid):
    b, t, d = x.shape
    n = b * t
    xf = x.reshape(n, d)
    u, vn, glu, q, gates = _in_proj(xf, wts["g_mix"], wts["w_a"], wts["w_b"], wts["w_cg"], wts["b_gate"],
                                    wts["ln_g"], wts["ln_b"], _tile_for(n, 512))
    r3 = lambda a: a.reshape(b, t, a.shape[-1])
    x1 = _mixer(x, r3(u), r3(vn), r3(glu), hist, r3(q), r3(gates), mem_k, mem_v, wts["ws"], wts["bs"],
                wts["w_a_out"], wts["conv_w"], wts["conv_b"], wts["conv_ln_g"], wts["conv_ln_b"], wts["w_b_out"],
                wts["w_c_out"], wts["w_o"], _tile_for(t, 256), min(t_valid, GMLP_CHUNK))
    x1f = x1[:, :t_valid].reshape(b * t_valid, d)
    nv = b * t_valid
    hn_t, sd, e1n, et2 = _route(x1f, wts["g_ffn"], wts["wq_t"], wts["k1"], wts["k2"], _tile_for(nv, 256))
    y = _experts(hn_t, sd, e1n, et2, wts["u"], wts["v_t"], x1f, wts["g_out"], _tile_for(nv, 512))
    return y.reshape(b, t_valid, d), r3(glu), r3(vn)


def kernel(x_prompt, mem_prompt, x_sample, cache_mem_k, cache_mem_v, state_conv, norm_mix_g, w_in, b_gate, gmlp_ln_g, gmlp_ln_b, gmlp_ws, gmlp_bs, w_a_out, conv_w, conv_b, conv_ln_g, conv_ln_b, w_b_out, norm_mem_g, w_mem_kv, w_c_out, w_o, norm_ffn_g, peer_wq, peer_k1, peer_k2, peer_u, peer_v, norm_out_g):
    assert w_in.shape[0] == 1, "single trunk layer"
    bp, tp, d = x_prompt.shape
    bs_, ts, _ = x_sample.shape
    m = mem_prompt.shape[1]
    hist_rows = CONV_K - 1
    assert tp % GMLP_CHUNK == 0 and ts <= CHUNK and ts <= hist_rows
    row = lambda a: a.reshape(1, -1)
    w_in_b = w_in[0].astype(BF16)
    wts = dict(
        g_mix=row(norm_mix_g[0]), w_a=w_in_b[:, :2 * d], w_b=w_in_b[:, 2 * d:4 * d], w_cg=w_in_b[:, 4 * d:],
        b_gate=row(b_gate[0]), ln_g=row(gmlp_ln_g[0]), ln_b=row(gmlp_ln_b[0]),
        ws=gmlp_ws[0], bs=gmlp_bs[0][:, :, None], w_a_out=w_a_out[0].astype(BF16), conv_w=conv_w[0],
        conv_b=row(conv_b[0]), conv_ln_g=row(conv_ln_g[0]), conv_ln_b=row(conv_ln_b[0]),
        w_b_out=w_b_out[0].astype(BF16), w_c_out=w_c_out[0].astype(BF16), w_o=w_o[0].astype(BF16),
        g_ffn=row(norm_ffn_g[0]), wq_t=peer_wq[0].T.astype(BF16), k1=peer_k1[0].astype(BF16),
        k2=peer_k2[0].astype(BF16), u=peer_u[0].astype(BF16), g_out=row(norm_out_g),
        v_t=jnp.swapaxes(peer_v[0].reshape(-1, CHUNK_KEYS * PEER_NKEYS, d), 1, 2).astype(BF16),
    )

    kv = _norm_matmul(mem_prompt.reshape(bp * m, d), row(norm_mem_g[0]), w_mem_kv[0].astype(BF16), _tile_for(bp * m, 512))
    kv = kv.reshape(bp, m, 2 * d)
    mk_p, mv_p = kv[:, :, :d], kv[:, :, d:]
    y_p, glu_p, _ = _layer(x_prompt, jnp.zeros((bp, HALO, d), F32), mk_p, mv_p, wts, tp)

    xs = jnp.pad(x_sample, ((0, 0), (0, GMLP_CHUNK - ts), (0, 0)))
    hist_s = jnp.pad(state_conv[0], ((0, 0), (HALO - hist_rows, 0), (0, 0)))
    y_s, glu_s, vn_s = _layer(xs, hist_s, cache_mem_k[0].reshape(bs_, m, d), cache_mem_v[0].reshape(bs_, m, d), wts, ts)

    hd = d // MEM_HEADS
    conv_s = jnp.concatenate([state_conv[0][:, ts:], glu_s[:, :ts]], axis=1)
    return (y_p, y_s, glu_p[:, tp - hist_rows:][None], mk_p.reshape(bp, m, MEM_HEADS, hd)[None],
            mv_p.reshape(bp, m, MEM_HEADS, hd)[None], conv_s[None], vn_s[:, :ts][None])
```
